```python
import jax, jax.numpy as jnp
from jax import lax
import numpy as np

D_MODEL = 1024
BATCH = 8
SEQ = 2048
DEPTH = 1

D_MIX = D_MODEL
D_LRU = D_MIX // 2
LRU_BLOCKS = 8
LRU_BLOCK_DIM = D_LRU // LRU_BLOCKS
CONV_WIDTH = 4
LRU_C = 8.0
D_ATTN = D_MIX - D_LRU
ATTN_HEADS = 8
HEAD_DIM = D_ATTN // ATTN_HEADS
Q_BLOCK = 128
D_IN_PROJ = 2 * D_LRU + 3 * D_ATTN + ATTN_HEADS
IN_SPLITS = (D_LRU, 2 * D_LRU, 2 * D_LRU + D_ATTN, 2 * D_LRU + 2 * D_ATTN, 2 * D_LRU + 3 * D_ATTN)
N_EXPERTS = 32
TOP_K = 4
D_EXPERT = D_MODEL
SWIGLU_ALPHA = 1.702
SWIGLU_LIMIT = 7.0
EXPERT_BLOCK = 256
N_MOD = 6
RMS_EPS = 1e-6

kernel_name = "hybrid_rglru_fox_moe_block"


def rmsnorm(x, g):
    xf = x.astype(jnp.float32)
    y = xf * lax.rsqrt(jnp.mean(xf * xf, axis=-1, keepdims=True) + RMS_EPS)
    return (y * g.astype(jnp.float32)).astype(x.dtype)


def causal_depthwise_conv(x, w, b):
    y = lax.conv_general_dilated(
        x, w[:, None, :].astype(x.dtype), window_strides=(1,),
        padding=[(CONV_WIDTH - 1, 0)], dimension_numbers=("NWC", "WIO", "NWC"),
        feature_group_count=x.shape[-1])
    return y + b


def _lin_rec_combine(left, right):
    a_l, b_l = left
    a_r, b_r = right
    return a_l * a_r, a_r * b_l + b_r


def rg_lru(x, wa, ba, wx, bx, lam):
    b, s, _ = x.shape
    xg = x.reshape(b, s, LRU_BLOCKS, LRU_BLOCK_DIM)
    r = jax.nn.sigmoid(jnp.einsum("bsgi,gij->bsgj", xg, wa) + ba).reshape(b, s, D_LRU)
    i = jax.nn.sigmoid(jnp.einsum("bsgi,gij->bsgj", xg, wx) + bx).reshape(b, s, D_LRU)
    log_a = -LRU_C * r.astype(jnp.float32) * jax.nn.softplus(-lam.astype(jnp.float32))
    a = jnp.exp(log_a)
    gated_x = jnp.sqrt(1.0 - jnp.exp(2.0 * log_a)) * (i * x).astype(jnp.float32)
    _, h = lax.associative_scan(_lin_rec_combine, (a, gated_x), axis=1)
    return h.astype(x.dtype)


def forgetting_attention(q, k, v, f_logit):
    b, s, h, dh = q.shape
    nb = s // Q_BLOCK
    cum = jnp.cumsum(jax.nn.log_sigmoid(f_logit.astype(jnp.float32)), axis=1).transpose(0, 2, 1)
    qb = q.reshape(b, nb, Q_BLOCK, h, dh).transpose(1, 0, 2, 3, 4)
    cq = cum.reshape(b, h, nb, Q_BLOCK).transpose(2, 0, 1, 3)
    kpos = jnp.arange(s, dtype=jnp.int32)
    qpos = kpos.reshape(nb, Q_BLOCK)
    scale = dh ** -0.5

    def block(args):
        q_blk, c_blk, p_blk = args
        logits = jnp.einsum("bqhd,bkhd->bhqk", q_blk, k).astype(jnp.float32) * scale
        logits = logits + c_blk[..., :, None] - cum[..., None, :]
        logits = jnp.where(p_blk[:, None] >= kpos[None, :], logits, -jnp.inf)
        p = jax.nn.softmax(logits, axis=-1).astype(v.dtype)
        return jnp.einsum("bhqk,bkhd->bqhd", p, v)

    out = lax.map(block, (qb, cq, qpos))
    return out.transpose(1, 0, 2, 3, 4).reshape(b, s, h * dh)


def hybrid_mixer(h, w_in, conv_w, conv_b, lru_wa, lru_ba, lru_wx, lru_bx, lru_lambda,
                 attn_fb, gn_lru, gn_attn, w_out):
    b, s, _ = h.shape
    proj = h @ w_in
    x_lru, y_lru, q, k, v, f_logit = jnp.split(proj, IN_SPLITS, axis=-1)
    x_conv = causal_depthwise_conv(x_lru, conv_w, conv_b)
    lru_out = rg_lru(x_conv, lru_wa, lru_ba, lru_wx, lru_bx, lru_lambda) * jax.nn.gelu(y_lru)
    shp = (b, s, ATTN_HEADS, HEAD_DIM)
    attn_out = forgetting_attention(q.reshape(shp), k.reshape(shp), v.reshape(shp), f_logit + attn_fb)
    merged = jnp.concatenate([rmsnorm(lru_out, gn_lru), rmsnorm(attn_out, gn_attn)], axis=-1)
    return merged @ w_out


def moe_ffn(h, w_router, b_router, w_up, b_up, w_down, b_down):
    n = h.shape[0]
    logits = (h @ w_router + b_router).astype(jnp.float32)
    top_logit, top_idx = lax.top_k(logits, TOP_K)
    top_w = jax.nn.softmax(top_logit, axis=-1).astype(h.dtype)
    e_flat = top_idx.reshape(-1).astype(jnp.int32)
    w_flat = top_w.reshape(-1)
    tok_flat = jnp.arange(n * TOP_K, dtype=jnp.int32) // TOP_K
    order = jnp.argsort(e_flat)
    e_sorted = e_flat[order]
    counts = jnp.bincount(e_flat, length=N_EXPERTS).astype(jnp.int32)
    starts = jnp.cumsum(counts) - counts
    padded = (counts + EXPERT_BLOCK - 1) // EXPERT_BLOCK * EXPERT_BLOCK
    pends = jnp.cumsum(padded)
    pstarts = pends - padded
    rank = jnp.arange(n * TOP_K, dtype=jnp.int32) - starts[e_sorted]
    dest = pstarts[e_sorted] + rank
    n_blocks = (n * TOP_K + EXPERT_BLOCK - 1) // EXPERT_BLOCK + N_EXPERTS
    n_rows = n_blocks * EXPERT_BLOCK
    row_tok = jnp.zeros((n_rows,), jnp.int32).at[dest].set(tok_flat[order])
    row_w = jnp.zeros((n_rows,), h.dtype).at[dest].set(w_flat[order])
    blk_start = jnp.arange(n_blocks, dtype=jnp.int32) * EXPERT_BLOCK
    blk_exp = jnp.minimum(jnp.searchsorted(pends, blk_start, side="right"), N_EXPERTS - 1).astype(jnp.int32)

    def expert_block(args):
        tok, e = args
        xb = h[tok]
        gu = xb @ w_up[e] + b_up[e]
        glu = jnp.minimum(gu[:, :D_EXPERT], SWIGLU_LIMIT)
        lin = jnp.clip(gu[:, D_EXPERT:], -SWIGLU_LIMIT, SWIGLU_LIMIT)
        act = glu * jax.nn.sigmoid(SWIGLU_ALPHA * glu) * (lin + 1.0)
        return act @ w_down[e] + b_down[e]

    rows = lax.map(expert_block, (row_tok.reshape(n_blocks, EXPERT_BLOCK), blk_exp))
    return jax.ops.segment_sum(rows.reshape(n_rows, -1) * row_w[:, None], row_tok, num_segments=n)


def setup_inputs(seed: int = 0) -> dict:
    key = jax.random.key(seed)
    ks = jax.random.split(key, 26)

    def nrm(k, shape, scale):
        return jax.random.normal(k, shape, jnp.float32) * scale

    u = jax.random.uniform(ks[13], (DEPTH, D_LRU), jnp.float32, 0.9, 0.999)
    p = u ** (1.0 / LRU_C)
    return {
        "x": nrm(ks[0], (BATCH, SEQ, D_MODEL), 1.0),
        "c": nrm(ks[1], (BATCH, D_MODEL), 1.0),
        "w_ada": nrm(ks[2], (DEPTH, D_MODEL, N_MOD * D_MODEL), 0.5 * D_MODEL ** -0.5),
        "b_ada": nrm(ks[3], (DEPTH, N_MOD * D_MODEL), 0.02),
        "norm_mix_pre": 1.0 + nrm(ks[4], (DEPTH, D_MODEL), 0.05),
        "norm_mix_post": 1.0 + nrm(ks[5], (DEPTH, D_MODEL), 0.05),
        "w_in": nrm(ks[6], (DEPTH, D_MODEL, D_IN_PROJ), D_MODEL ** -0.5),
        "conv_w": nrm(ks[7], (DEPTH, CONV_WIDTH, D_LRU), CONV_WIDTH ** -0.5),
        "conv_b": nrm(ks[8], (DEPTH, D_LRU), 0.02),
        "lru_wa": nrm(ks[9], (DEPTH, LRU_BLOCKS, LRU_BLOCK_DIM, LRU_BLOCK_DIM), LRU_BLOCK_DIM ** -0.5),
        "lru_ba": nrm(ks[10], (DEPTH, LRU_BLOCKS, LRU_BLOCK_DIM), 0.02),
        "lru_wx": nrm(ks[11], (DEPTH, LRU_BLOCKS, LRU_BLOCK_DIM, LRU_BLOCK_DIM), LRU_BLOCK_DIM ** -0.5),
        "lru_bx": nrm(ks[12], (DEPTH, LRU_BLOCKS, LRU_BLOCK_DIM), 0.02),
        "lru_lambda": jnp.log(p) - jnp.log1p(-p),
        "attn_fb": jnp.linspace(1.0, 5.0, ATTN_HEADS, dtype=jnp.float32)[None, :] + nrm(ks[14], (DEPTH, ATTN_HEADS), 0.1),
        "gn_lru": 1.0 + nrm(ks[15], (DEPTH, D_LRU), 0.05),
        "gn_attn": 1.0 + nrm(ks[16], (DEPTH, D_ATTN), 0.05),
        "w_out": nrm(ks[17], (DEPTH, D_MIX, D_MODEL), D_MIX ** -0.5),
        "norm_ffn_pre": 1.0 + nrm(ks[18], (DEPTH, D_MODEL), 0.05),
        "norm_ffn_post": 1.0 + nrm(ks[19], (DEPTH, D_MODEL), 0.05),
        "w_router": nrm(ks[20], (DEPTH, D_MODEL, N_EXPERTS), D_MODEL ** -0.5),
        "b_router": nrm(ks[21], (DEPTH, N_EXPERTS), 0.01),
        "w_up": nrm(ks[22], (DEPTH, N_EXPERTS, D_MODEL, 2 * D_EXPERT), D_MODEL ** -0.5),
        "b_up": nrm(ks[23], (DEPTH, N_EXPERTS, 2 * D_EXPERT), 0.01),
        "w_down": nrm(ks[24], (DEPTH, N_EXPERTS, D_EXPERT, D_MODEL), D_EXPERT ** -0.5),
        "b_down": nrm(ks[25], (DEPTH, N_EXPERTS, D_MODEL), 0.01),
    }


def reference(x, c, w_ada, b_ada, norm_mix_pre, norm_mix_post, w_in, conv_w, conv_b,
              lru_wa, lru_ba, lru_wx, lru_bx, lru_lambda, attn_fb, gn_lru, gn_attn, w_out,
              norm_ffn_pre, norm_ffn_post, w_router, b_router, w_up, b_up, w_down, b_down):
    b, s, d = x.shape
    cond = jax.nn.silu(c)
    for l in range(DEPTH):
        mod = cond @ w_ada[l] + b_ada[l]
        sh_m, sc_m, g_m, sh_f, sc_f, g_f = [m[:, None, :] for m in jnp.split(mod, N_MOD, axis=-1)]
        h = rmsnorm(x, norm_mix_pre[l]) * (1.0 + sc_m) + sh_m
        y = hybrid_mixer(h, w_in[l], conv_w[l], conv_b[l], lru_wa[l], lru_ba[l], lru_wx[l],
                         lru_bx[l], lru_lambda[l], attn_fb[l], gn_lru[l], gn_attn[l], w_out[l])
        x = x + g_m * rmsnorm(y, norm_mix_post[l])
        h = rmsnorm(x, norm_ffn_pre[l]) * (1.0 + sc_f) + sh_f
        y = moe_ffn(h.reshape(b * s, d), w_router[l], b_router[l], w_up[l], b_up[l],
                    w_down[l], b_down[l]).reshape(b, s, d)
        x = x + g_f * rmsnorm(y, norm_ffn_post[l])
    return x
```

```python
import functools

import jax
import jax.numpy as jnp
from jax import lax
from jax.experimental import pallas as pl
from jax.experimental.pallas import tpu as pltpu

F32 = jnp.float32
BF16 = jnp.bfloat16

D_MODEL = 1024
D_LRU = 512
LRU_BLOCKS = 8
CONV_WIDTH = 4
LRU_C = 8.0
D_ATTN = 512
ATTN_HEADS = 8
HEAD_DIM = 64
N_EXPERTS = 32
TOP_K = 4
D_EXPERT = 1024
SWIGLU_ALPHA = 1.702
SWIGLU_LIMIT = 7.0
N_MOD = 6
RMS_EPS = 1e-6

LANES = 128
SUBLANES = 8
VMEM_LIMIT = 48 * 1024 * 1024

TOKEN_TILE = 512
SCAN_TILE = 256
ATTN_TILE = 256
ROUTE_TILE = 512
ROW_BLOCK = 256
COMBINE_TILE = 256
NEG_BIG = -1e30


def _sigmoid(z):
    return 1.0 / (1.0 + jnp.exp(-z))


def _rms(x, g):
    ms = jnp.mean(x * x, axis=-1, keepdims=True)
    return x * lax.rsqrt(ms + RMS_EPS) * g


def _params(sem):
    return pltpu.CompilerParams(dimension_semantics=sem, vmem_limit_bytes=VMEM_LIMIT)


def _ada_kernel(c_ref, w_ref, b_ref, o_ref):
    c = c_ref[...]
    cond = c * _sigmoid(c)
    o_ref[...] = jnp.dot(cond, w_ref[...], preferred_element_type=F32,
                         precision=lax.Precision.HIGHEST) + b_ref[...]


def _ada(c, w, b):
    bsz = c.shape[0]
    n_out = w.shape[1]
    return pl.pallas_call(
        _ada_kernel,
        grid=(n_out // D_MODEL,),
        in_specs=[pl.BlockSpec((bsz, D_MODEL), lambda j: (0, 0)),
                  pl.BlockSpec((D_MODEL, D_MODEL), lambda j: (0, j)),
                  pl.BlockSpec((1, D_MODEL), lambda j: (0, j))],
        out_specs=pl.BlockSpec((bsz, D_MODEL), lambda j: (0, j)),
        out_shape=jax.ShapeDtypeStruct((bsz, n_out), F32),
        compiler_params=_params(("arbitrary",)),
        name="ada_mod",
    )(c, w, b.reshape(1, n_out))


def _inproj_kernel(x_ref, mod_ref, g_ref, w_ref, xl_ref, yl_ref, q_ref, k_ref, v_ref, f_ref):
    x = x_ref[...]
    m = mod_ref[0]
    sh = m[:, 0:D_MODEL]
    sc = m[:, D_MODEL:2 * D_MODEL]
    h = _rms(x, g_ref[...]) * (1.0 + sc) + sh
    hb = h.astype(BF16)

    def proj(lo, hi):
        return jnp.dot(hb, w_ref[:, lo:hi], preferred_element_type=F32)

    xl_ref[...] = proj(0, D_LRU)
    yl_ref[...] = proj(D_LRU, 2 * D_LRU)
    o = 2 * D_LRU
    q_ref[...] = (proj(o, o + D_ATTN) * (HEAD_DIM ** -0.5)).astype(BF16)
    k_ref[...] = proj(o + D_ATTN, o + 2 * D_ATTN).astype(BF16)
    v_ref[...] = proj(o + 2 * D_ATTN, o + 3 * D_ATTN).astype(BF16)
    f_ref[...] = proj(o + 3 * D_ATTN, o + 3 * D_ATTN + LANES)


def _inproj(x2d, mod3, g, w_in_b, seq):
    n = x2d.shape[0]
    tpb = seq // TOKEN_TILE
    wcols = w_in_b.shape[1]
    row = lambda i: (i, 0)
    return pl.pallas_call(
        _inproj_kernel,
        grid=(n // TOKEN_TILE,),
        in_specs=[pl.BlockSpec((TOKEN_TILE, D_MODEL), row),
                  pl.BlockSpec((1, 1, N_MOD * D_MODEL), lambda i: (i // tpb, 0, 0)),
                  pl.BlockSpec((1, D_MODEL), lambda i: (0, 0)),
                  pl.BlockSpec((D_MODEL, wcols), lambda i: (0, 0))],
        out_specs=[pl.BlockSpec((TOKEN_TILE, D_LRU), row),
                   pl.BlockSpec((TOKEN_TILE, D_LRU), row),
                   pl.BlockSpec((TOKEN_TILE, D_ATTN), row),
                   pl.BlockSpec((TOKEN_TILE, D_ATTN), row),
                   pl.BlockSpec((TOKEN_TILE, D_ATTN), row),
                   pl.BlockSpec((TOKEN_TILE, LANES), row)],
        out_shape=[jax.ShapeDtypeStruct((n, D_LRU), F32),
                   jax.ShapeDtypeStruct((n, D_LRU), F32),
                   jax.ShapeDtypeStruct((n, D_ATTN), BF16),
                   jax.ShapeDtypeStruct((n, D_ATTN), BF16),
                   jax.ShapeDtypeStruct((n, D_ATTN), BF16),
                   jax.ShapeDtypeStruct((n, LANES), F32)],
        compiler_params=_params(("arbitrary",)),
        name="in_proj",
    )(x2d, mod3, g, w_in_b)


def _shift_rows(x, d, fill):
    rows = lax.broadcasted_iota(jnp.int32, x.shape, 0)
    return jnp.where(rows >= d, pltpu.roll(x, d, 0), fill)


def _lru_kernel(xl_ref, yl_ref, f_ref, cw_ref, cb_ref, wg_ref, bg_ref, lam_ref, fb_ref, gn_ref,
                o_ref, ccol_ref, crow_ref, prev_ref, hcar_ref, ccar_ref):
    j = pl.program_id(1)

    @pl.when(j == 0)
    def _():
        prev_ref[...] = jnp.zeros_like(prev_ref)
        hcar_ref[...] = jnp.zeros_like(hcar_ref)
        ccar_ref[...] = jnp.zeros_like(ccar_ref)

    x = xl_ref[...]
    t = x.shape[0]
    prev = prev_ref[...]
    rows8 = lax.broadcasted_iota(jnp.int32, prev.shape, 0)
    cw = cw_ref[...]
    xc = x * cw[CONV_WIDTH - 1:CONV_WIDTH, :] + cb_ref[...]
    for d in range(1, CONV_WIDTH):
        rolled = pltpu.roll(x, d, 0)
        head = jnp.where(rows8 < d, pltpu.roll(prev, d, 0), rolled[:SUBLANES])
        shifted = jnp.concatenate([head, rolled[SUBLANES:]], axis=0)
        xc = xc + shifted * cw[CONV_WIDTH - 1 - d:CONV_WIDTH - d, :]
    prev_ref[...] = x[t - SUBLANES:]

    gates = jnp.dot(xc.astype(BF16), wg_ref[...], preferred_element_type=F32) + bg_ref[...]
    r = _sigmoid(gates[:, :D_LRU])
    i = _sigmoid(gates[:, D_LRU:])
    nlam = -lam_ref[...]
    softplus = jnp.maximum(nlam, 0.0) + jnp.log1p(jnp.exp(-jnp.abs(nlam)))
    log_a = (-LRU_C) * r * softplus
    a = jnp.exp(log_a)
    b = jnp.sqrt(1.0 - jnp.exp(2.0 * log_a)) * (i * xc)

    d = 1
    while d < t:
        b = a * _shift_rows(b, d, 0.0) + b
        a = a * _shift_rows(a, d, 1.0)
        d *= 2
    h = b + a * hcar_ref[SUBLANES - 1:SUBLANES, :]
    hcar_ref[...] = h[t - SUBLANES:]

    y = yl_ref[...]
    gelu = 0.5 * y * (1.0 + jnp.tanh(0.7978845608028654 * (y + 0.044715 * (y * y * y))))
    o_ref[...] = _rms(h * gelu, gn_ref[...]).astype(o_ref.dtype)

    z = f_ref[...] + fb_ref[...]
    c = jnp.minimum(z, 0.0) - jnp.log1p(jnp.exp(-jnp.abs(z)))
    d = 1
    while d < t:
        c = c + _shift_rows(c, d, 0.0)
        d *= 2
    c = c + ccar_ref[SUBLANES - 1:SUBLANES, :]
    ccar_ref[...] = c[t - SUBLANES:]
    ccol_ref[...] = c
    crow_ref[0] = c.T[:SUBLANES, :]


def _lru(xl, yl, f, cw, cb, wg, bg, lam, fb, gn, bsz, seq):
    n = xl.shape[0]
    nt = seq // SCAN_TILE
    row = lambda b, j: (b * nt + j, 0)
    const = lambda b, j: (0, 0)
    return pl.pallas_call(
        _lru_kernel,
        grid=(bsz, nt),
        in_specs=[pl.BlockSpec((SCAN_TILE, D_LRU), row),
                  pl.BlockSpec((SCAN_TILE, D_LRU), row),
                  pl.BlockSpec((SCAN_TILE, LANES), row),
                  pl.BlockSpec((CONV_WIDTH, D_LRU), const),
                  pl.BlockSpec((1, D_LRU), const),
                  pl.BlockSpec((D_LRU, 2 * D_LRU), const),
                  pl.BlockSpec((1, 2 * D_LRU), const),
                  pl.BlockSpec((1, D_LRU), const),
                  pl.BlockSpec((1, LANES), const),
                  pl.BlockSpec((1, D_LRU), const)],
        out_specs=[pl.BlockSpec((SCAN_TILE, D_LRU), row),
                   pl.BlockSpec((SCAN_TILE, LANES), row),
                   pl.BlockSpec((1, SUBLANES, SCAN_TILE), lambda b, j: (b, 0, j))],
        out_shape=[jax.ShapeDtypeStruct((n, D_LRU), BF16),
                   jax.ShapeDtypeStruct((n, LANES), F32),
                   jax.ShapeDtypeStruct((bsz, SUBLANES, seq), F32)],
        scratch_shapes=[pltpu.VMEM((SUBLANES, D_LRU), F32),
                        pltpu.VMEM((SUBLANES, D_LRU), F32),
                        pltpu.VMEM((SUBLANES, LANES), F32)],
        compiler_params=_params(("arbitrary", "arbitrary")),
        name="rg_lru",
    )(xl, yl, f, cw, cb, wg, bg, lam, fb, gn)


def _attn_kernel(q_ref, k_ref, v_ref, ccol_ref, crow_ref, o_ref):
    hg = pl.program_id(1)
    qi = pl.program_id(2)
    tq = q_ref.shape[0]
    q2 = q_ref[...]
    lane = lax.broadcasted_iota(jnp.int32, q2.shape, 1)
    ccol = ccol_ref[...]
    rows = lax.broadcasted_iota(jnp.int32, (tq, tq), 0)
    cols = lax.broadcasted_iota(jnp.int32, (tq, tq), 1)
    causal = cols <= rows
    nt_dims = (((1,), (1,)), ((), ()))

    outs = []
    for half in range(2):
        head = 2 * hg + half
        in_half = (lane >= half * HEAD_DIM) & (lane < (half + 1) * HEAD_DIM)
        qh = jnp.where(in_half, q2, jnp.zeros_like(q2))
        cq = jnp.sum(jnp.where(lane == head, ccol, 0.0), axis=-1, keepdims=True)

        def step(jk, carry, masked):
            m, l, acc = carry
            start = pl.multiple_of(jk * tq, tq)
            k2 = k_ref[pl.ds(start, tq), :]
            v2 = v_ref[pl.ds(start, tq), :]
            ck = crow_ref[0, pl.ds(head, 1), pl.ds(start, tq)]
            s = lax.dot_general(qh, k2, nt_dims, preferred_element_type=F32) - ck
            if masked:
                s = jnp.where(causal, s, NEG_BIG)
            m_new = jnp.maximum(m, jnp.max(s, axis=-1, keepdims=True) + cq)
            p = jnp.exp(s - (m_new - cq))
            alpha = jnp.exp(m - m_new)
            l_new = alpha * l + jnp.sum(p, axis=-1, keepdims=True)
            acc_new = alpha * acc + jnp.dot(p.astype(BF16), v2, preferred_element_type=F32)
            return m_new, l_new, acc_new

        init = (jnp.full((tq, 1), NEG_BIG, F32), jnp.zeros((tq, 1), F32),
                jnp.zeros((tq, LANES), F32))
        carry = lax.fori_loop(0, qi, functools.partial(step, masked=False), init)
        m, l, acc = step(qi, carry, masked=True)
        outs.append(acc / l)

    o_ref[...] = jnp.where(lane < HEAD_DIM, outs[0], outs[1])


def _attention(q, k, v, ccol, crow, bsz, seq):
    n = q.shape[0]
    nq = seq // ATTN_TILE
    qmap = lambda b, g, i: (b * nq + i, g)
    kvmap = lambda b, g, i: (b, g)
    return pl.pallas_call(
        _attn_kernel,
        grid=(bsz, D_ATTN // LANES, nq),
        in_specs=[pl.BlockSpec((ATTN_TILE, LANES), qmap),
                  pl.BlockSpec((seq, LANES), kvmap),
                  pl.BlockSpec((seq, LANES), kvmap),
                  pl.BlockSpec((ATTN_TILE, LANES), lambda b, g, i: (b * nq + i, 0)),
                  pl.BlockSpec((1, SUBLANES, seq), lambda b, g, i: (b, 0, 0))],
        out_specs=pl.BlockSpec((ATTN_TILE, LANES), qmap),
        out_shape=jax.ShapeDtypeStruct((n, D_ATTN), F32),
        compiler_params=_params(("arbitrary", "arbitrary", "arbitrary")),
        name="fox_attention",
    )(q, k, v, ccol, crow)


def _outproj_kernel(lru_ref, attn_ref, x_ref, mod_ref, gna_ref, wo_ref, gpost_ref, gpre_ref,
                    wr_ref, br_ref, x1_ref, h2_ref, idx_ref, tw_ref):
    m = mod_ref[0]
    g_m = m[:, 2 * D_MODEL:3 * D_MODEL]
    sh_f = m[:, 3 * D_MODEL:4 * D_MODEL]
    sc_f = m[:, 4 * D_MODEL:5 * D_MODEL]
    attn_n = _rms(attn_ref[...], gna_ref[...]).astype(BF16)
    y = (jnp.dot(lru_ref[...], wo_ref[0:D_LRU, :], preferred_element_type=F32)
         + jnp.dot(attn_n, wo_ref[D_LRU:, :], preferred_element_type=F32))
    x1 = x_ref[...] + g_m * _rms(y, gpost_ref[...])
    x1_ref[...] = x1
    h2 = _rms(x1, gpre_ref[...]) * (1.0 + sc_f) + sh_f
    h2_ref[...] = h2

    logits = jnp.dot(h2, wr_ref[...], preferred_element_type=F32,
                     precision=lax.Precision.HIGHEST) + br_ref[...]
    lane = lax.broadcasted_iota(jnp.int32, logits.shape, 1)
    lane_f = lane.astype(F32)
    cur = logits
    vals, idxs = [], []
    for _ in range(TOP_K):
        mx = jnp.max(cur, axis=-1, keepdims=True)
        ix = jnp.min(jnp.where(cur == mx, lane_f, float(LANES)), axis=-1, keepdims=True)
        vals.append(mx)
        idxs.append(ix)
        cur = jnp.where(lane_f == ix, -jnp.inf, cur)
    exps = [jnp.exp(vv - vals[0]) for vv in vals]
    den = exps[0] + exps[1] + exps[2] + exps[3]
    idx_out = jnp.zeros(logits.shape, F32)
    w_out = jnp.zeros(logits.shape, F32)
    for kk in range(TOP_K):
        idx_out = jnp.where(lane == kk, idxs[kk], idx_out)
        w_out = jnp.where(lane == kk, exps[kk] / den, w_out)
    idx_ref[...] = idx_out.astype(jnp.int32)
    tw_ref[...] = w_out


def _outproj(lru_n, attn, x2d, mod3, gna, wo_b, gpost, gpre, wr, br, seq):
    n = x2d.shape[0]
    tpb = seq // TOKEN_TILE
    row = lambda i: (i, 0)
    const = lambda i: (0, 0)
    return pl.pallas_call(
        _outproj_kernel,
        grid=(n // TOKEN_TILE,),
        in_specs=[pl.BlockSpec((TOKEN_TILE, D_LRU), row),
                  pl.BlockSpec((TOKEN_TILE, D_ATTN), row),
                  pl.BlockSpec((TOKEN_TILE, D_MODEL), row),
                  pl.BlockSpec((1, 1, N_MOD * D_MODEL), lambda i: (i // tpb, 0, 0)),
                  pl.BlockSpec((1, D_ATTN), const),
                  pl.BlockSpec((D_MODEL, D_MODEL), const),
                  pl.BlockSpec((1, D_MODEL), const),
                  pl.BlockSpec((1, D_MODEL), const),
                  pl.BlockSpec((D_MODEL, LANES), const),
                  pl.BlockSpec((1, LANES), const)],
        out_specs=[pl.BlockSpec((TOKEN_TILE, D_MODEL), row),
                   pl.BlockSpec((TOKEN_TILE, D_MODEL), row),
                   pl.BlockSpec((TOKEN_TILE, LANES), row),
                   pl.BlockSpec((TOKEN_TILE, LANES), row)],
        out_shape=[jax.ShapeDtypeStruct((n, D_MODEL), F32),
                   jax.ShapeDtypeStruct((n, D_MODEL), F32),
                   jax.ShapeDtypeStruct((n, LANES), jnp.int32),
                   jax.ShapeDtypeStruct((n, LANES), F32)],
        compiler_params=_params(("arbitrary",)),
        name="out_proj_router",
    )(lru_n, attn, x2d, mod3, gna, wo_b, gpost, gpre, wr, br)


def _rank_kernel(idx_ref, rank_ref, cnt_ref, base_ref):
    i = pl.program_id(0)

    @pl.when(i == 0)
    def _():
        base_ref[...] = jnp.zeros_like(base_ref)

    idx = idx_ref[...]
    t = idx.shape[0]
    lane = lax.broadcasted_iota(jnp.int32, idx.shape, 1)
    onehots = [lane == idx[:, kk:kk + 1] for kk in range(TOP_K)]
    tot = jnp.zeros(idx.shape, F32)
    for oh in onehots:
        tot = tot + jnp.where(oh, 1.0, 0.0)
    rows = lax.broadcasted_iota(jnp.int32, (t, t), 0)
    cols = lax.broadcasted_iota(jnp.int32, (t, t), 1)
    lower = jnp.where(cols < rows, 1.0, 0.0).astype(BF16)
    before = jnp.dot(lower, tot.astype(BF16), preferred_element_type=F32)
    base = base_ref[0:1, :]
    seen = before + base
    rank = jnp.zeros(idx.shape, F32)
    for kk, oh in enumerate(onehots):
        rk = jnp.sum(jnp.where(oh, seen, 0.0), axis=-1, keepdims=True)
        rank = jnp.where(lane == kk, rk, rank)
    rank_ref[...] = rank.astype(jnp.int32)
    new_base = base + jnp.sum(tot, axis=0, keepdims=True)
    base_ref[...] = jnp.broadcast_to(new_base, base_ref.shape)
    cnt_ref[...] = jnp.broadcast_to(new_base, cnt_ref.shape).astype(jnp.int32)


def _ranks(idx):
    n = idx.shape[0]
    return pl.pallas_call(
        _rank_kernel,
        grid=(n // ROUTE_TILE,),
        in_specs=[pl.BlockSpec((ROUTE_TILE, LANES), lambda i: (i, 0))],
        out_specs=[pl.BlockSpec((ROUTE_TILE, LANES), lambda i: (i, 0)),
                   pl.BlockSpec((SUBLANES, LANES), lambda i: (0, 0))],
        out_shape=[jax.ShapeDtypeStruct((n, LANES), jnp.int32),
                   jax.ShapeDtypeStruct((SUBLANES, LANES), jnp.int32)],
        scratch_shapes=[pltpu.VMEM((SUBLANES, LANES), F32)],
        compiler_params=_params(("arbitrary",)),
        name="route_ranks",
    )(idx)


def _expert_kernel(bexp_ref, nused_ref, inv_ref, h2_hbm, wup_ref, bup_ref, wdn_ref, bdn_ref,
                   out_hbm, xbuf, ybuf, wup_b, wdn_b, gsem, ssem, *, n_tokens):
    b = pl.program_id(0)

    @pl.when(b < nused_ref[0])
    def _():
        e = bexp_ref[b]
        e_prev = bexp_ref[jnp.maximum(b - 1, 0)]

        @pl.when((b == 0) | (e != e_prev))
        def _():
            wup_b[...] = wup_ref[0].astype(BF16)
            wdn_b[...] = wdn_ref[0].astype(BF16)

        base = b * ROW_BLOCK

        def gather_copy(r):
            pair = inv_ref[base + r]
            tok = lax.rem(pair, n_tokens)
            return pltpu.make_async_copy(h2_hbm.at[pl.ds(tok, 1), :],
                                         xbuf.at[pl.ds(r, 1), :], gsem)

        def scatter_copy(pair, r):
            return pltpu.make_async_copy(ybuf.at[pl.ds(r, 1), :],
                                         out_hbm.at[pl.ds(pair, 1), :], ssem)

        def start_gather(r, c):
            gather_copy(r).start()
            return c

        def wait_gather(r, c):
            gather_copy(r).wait()
            return c

        lax.fori_loop(0, ROW_BLOCK, start_gather, 0)
        lax.fori_loop(0, ROW_BLOCK, wait_gather, 0)

        xb = xbuf[...].astype(BF16)
        gu = jnp.dot(xb, wup_b[...], preferred_element_type=F32) + bup_ref[0]
        glu = jnp.minimum(gu[:, :D_EXPERT], SWIGLU_LIMIT)
        lin = jnp.clip(gu[:, D_EXPERT:], -SWIGLU_LIMIT, SWIGLU_LIMIT)
        act = glu * _sigmoid(SWIGLU_ALPHA * glu) * (lin + 1.0)
        ybuf[...] = jnp.dot(act.astype(BF16), wdn_b[...], preferred_element_type=F32) + bdn_ref[0]

        def start_scatter(r, c):
            pair = inv_ref[base + r]

            @pl.when(pair < TOP_K * n_tokens)
            def _():
                scatter_copy(pair, r).start()
            return c

        def wait_scatter(r, c):
            pair = inv_ref[base + r]

            @pl.when(pair < TOP_K * n_tokens)
            def _():
                scatter_copy(pair, r).wait()
            return c

        lax.fori_loop(0, ROW_BLOCK, start_scatter, 0)
        lax.fori_loop(0, ROW_BLOCK, wait_scatter, 0)


def _experts(bexp, nused, inv, h2, w_up, b_up, w_down, b_down):
    n = h2.shape[0]
    n_blocks = bexp.shape[0]
    n_out_rows = n * TOP_K
    grid_spec = pltpu.PrefetchScalarGridSpec(
        num_scalar_prefetch=3,
        grid=(n_blocks,),
        in_specs=[pl.BlockSpec(memory_space=pl.ANY),
                  pl.BlockSpec((1, D_MODEL, 2 * D_EXPERT), lambda b, be, nu, iv: (be[b], 0, 0)),
                  pl.BlockSpec((1, 1, 2 * D_EXPERT), lambda b, be, nu, iv: (be[b], 0, 0)),
                  pl.BlockSpec((1, D_EXPERT, D_MODEL), lambda b, be, nu, iv: (be[b], 0, 0)),
                  pl.BlockSpec((1, 1, D_MODEL), lambda b, be, nu, iv: (be[b], 0, 0))],
        out_specs=pl.BlockSpec(memory_space=pl.ANY),
        scratch_shapes=[pltpu.VMEM((ROW_BLOCK, D_MODEL), F32),
                        pltpu.VMEM((ROW_BLOCK, D_MODEL), F32),
                        pltpu.VMEM((D_MODEL, 2 * D_EXPERT), BF16),
                        pltpu.VMEM((D_EXPERT, D_MODEL), BF16),
                        pltpu.SemaphoreType.DMA(()),
                        pltpu.SemaphoreType.DMA(())],
    )
    return pl.pallas_call(
        functools.partial(_expert_kernel, n_tokens=n),
        grid_spec=grid_spec,
        out_shape=jax.ShapeDtypeStruct((n_out_rows, D_MODEL), F32),
        compiler_params=_params(("arbitrary",)),
        name="moe_experts",
    )(bexp, nused, inv, h2, w_up, b_up.reshape(N_EXPERTS, 1, 2 * D_EXPERT),
      w_down, b_down.reshape(N_EXPERTS, 1, D_MODEL))


def _combine_kernel(r0_ref, r1_ref, r2_ref, r3_ref, tw_ref, x1_ref, mod_ref, g_ref, o_ref):
    tw = tw_ref[...]
    y = r0_ref[...] * tw[:, 0:1]
    for kk, r_ref in enumerate((r1_ref, r2_ref, r3_ref), start=1):
        y = y + r_ref[...] * tw[:, kk:kk + 1]
    g_f = mod_ref[0][:, 5 * D_MODEL:6 * D_MODEL]
    o_ref[...] = x1_ref[...] + g_f * _rms(y, g_ref[...])


def _combine(rows, tw, x1, mod3, g, seq):
    n = x1.shape[0]
    tpb = seq // COMBINE_TILE
    nt = n // COMBINE_TILE
    row = lambda i: (i, 0)
    slot_specs = [pl.BlockSpec((COMBINE_TILE, D_MODEL), functools.partial(lambda i, kk: (kk * nt + i, 0), kk=kk))
                  for kk in range(TOP_K)]
    return pl.pallas_call(
        _combine_kernel,
        grid=(nt,),
        in_specs=slot_specs + [
            pl.BlockSpec((COMBINE_TILE, LANES), row),
            pl.BlockSpec((COMBINE_TILE, D_MODEL), row),
            pl.BlockSpec((1, 1, N_MOD * D_MODEL), lambda i: (i // tpb, 0, 0)),
            pl.BlockSpec((1, D_MODEL), lambda i: (0, 0))],
        out_specs=pl.BlockSpec((COMBINE_TILE, D_MODEL), row),
        out_shape=jax.ShapeDtypeStruct((n, D_MODEL), F32),
        compiler_params=_params(("arbitrary",)),
        name="moe_combine",
    )(rows, rows, rows, rows, tw, x1, mod3, g)


def _block_diag(w):
    g, d, _ = w.shape
    eye = jnp.eye(g, dtype=w.dtype)
    return jnp.einsum("gij,gh->gihj", w, eye).reshape(g * d, g * d)


def _layer(x2d, mod, bsz, seq, norm_mix_pre, norm_mix_post, w_in, conv_w, conv_b, lru_wa, lru_ba,
           lru_wx, lru_bx, lru_lambda, attn_fb, gn_lru, gn_attn, w_out, norm_ffn_pre, norm_ffn_post,
           w_router, b_router, w_up, b_up, w_down, b_down):
    n = x2d.shape[0]
    mod3 = mod.reshape(bsz, 1, N_MOD * D_MODEL)
    d_in = w_in.shape[1]
    w_in_b = jnp.pad(w_in, ((0, 0), (0, 2 * D_LRU + 3 * D_ATTN + LANES - d_in))).astype(BF16)
    xl, yl, q, k, v, f = _inproj(x2d, mod3, norm_mix_pre.reshape(1, -1), w_in_b, seq)

    wg = jnp.concatenate([_block_diag(lru_wa), _block_diag(lru_wx)], axis=1).astype(BF16)
    bg = jnp.concatenate([lru_ba.reshape(1, -1), lru_bx.reshape(1, -1)], axis=1)
    fb = jnp.pad(attn_fb.reshape(1, -1), ((0, 0), (0, LANES - ATTN_HEADS)))
    lru_n, ccol, crow = _lru(xl, yl, f, conv_w, conv_b.reshape(1, -1), wg, bg,
                             lru_lambda.reshape(1, -1), fb, gn_lru.reshape(1, -1), bsz, seq)

    attn = _attention(q, k, v, ccol, crow, bsz, seq)

    wr = jnp.pad(w_router, ((0, 0), (0, LANES - N_EXPERTS)))
    br = jnp.pad(b_router.reshape(1, -1), ((0, 0), (0, LANES - N_EXPERTS)), constant_values=NEG_BIG)
    x1, h2, idx, tw = _outproj(lru_n, attn, x2d, mod3, gn_attn.reshape(1, -1), w_out.astype(BF16),
                               norm_mix_post.reshape(1, -1), norm_ffn_pre.reshape(1, -1), wr, br, seq)

    rank, cnt = _ranks(idx)
    counts = cnt[0, :N_EXPERTS]
    padded = (counts + ROW_BLOCK - 1) // ROW_BLOCK * ROW_BLOCK
    pends = jnp.cumsum(padded)
    pstarts = pends - padded
    n_blocks = n * TOP_K // ROW_BLOCK + N_EXPERTS
    n_rows = n_blocks * ROW_BLOCK
    blk_start = jnp.arange(n_blocks, dtype=jnp.int32) * ROW_BLOCK
    bexp = jnp.minimum(jnp.searchsorted(pends, blk_start, side="right"), N_EXPERTS - 1).astype(jnp.int32)
    nused = (pends[-1] // ROW_BLOCK).astype(jnp.int32).reshape(1)
    e_flat = idx[:, :TOP_K].T.reshape(-1)
    dest = pstarts[e_flat] + rank[:, :TOP_K].T.reshape(-1)
    spare = n * TOP_K + jnp.arange(n_rows, dtype=jnp.int32) % ROW_BLOCK
    inv = spare.at[dest].set(jnp.arange(n * TOP_K, dtype=jnp.int32))

    rows = _experts(bexp, nused, inv, h2, w_up, b_up, w_down, b_down)
    return _combine(rows, tw, x1, mod3, norm_ffn_post.reshape(1, -1), seq)


def kernel(x, c, w_ada, b_ada, norm_mix_pre, norm_mix_post, w_in, conv_w, conv_b, lru_wa, lru_ba,
           lru_wx, lru_bx, lru_lambda, attn_fb, gn_lru, gn_attn, w_out, norm_ffn_pre, norm_ffn_post,
           w_router, b_router, w_up, b_up, w_down, b_down):
    bsz, seq, d = x.shape
    x2d = x.reshape(bsz * seq, d)
    for l in range(w_ada.shape[0]):
        mod = _ada(c, w_ada[l], b_ada[l])
        x2d = _layer(x2d, mod, bsz, seq, norm_mix_pre[l], norm_mix_post[l], w_in[l], conv_w[l],
                     conv_b[l], lru_wa[l], lru_ba[l], lru_wx[l], lru_bx[l], lru_lambda[l],
                     attn_fb[l], gn_lru[l], gn_attn[l], w_out[l], norm_ffn_pre[l], norm_ffn_post[l],
                     w_router[l], b_router[l], w_up[l], b_up[l], w_down[l], b_down[l])
    return x2d.reshape(bsz, seq, d)
```

```python
import functools

import jax
import jax.numpy as jnp
from jax import lax
from jax.experimental import pallas as pl
from jax.experimental.pallas import tpu as pltpu

F32 = jnp.float32
BF16 = jnp.bfloat16

D_MODEL = 1024
D_LRU = 512
LRU_BLOCKS = 8
CONV_WIDTH = 4
LRU_C = 8.0
D_ATTN = 512
ATTN_HEADS = 8
HEAD_DIM = 64
N_EXPERTS = 32
TOP_K = 4
D_EXPERT = 1024
SWIGLU_ALPHA = 1.702
SWIGLU_LIMIT = 7.0
N_MOD = 6
RMS_EPS = 1e-6

LANES = 128
SUBLANES = 8
MXU_TILE = 256
VMEM_LIMIT = 48 * 1024 * 1024

TOKEN_TILE = 512
SCAN_TILE = 256
ATTN_TILE = 256
ROUTE_TILE = 512
ROW_BLOCK = 256
COMBINE_TILE = 256
NEG_BIG = -1e30
ROW_CHUNKS = D_MODEL // LANES
assert ROW_CHUNKS == SUBLANES


def _sigmoid(z):
    return 1.0 / (1.0 + jnp.exp(-z))


def _rms(x, g):
    ms = jnp.mean(x * x, axis=-1, keepdims=True)
    return x * lax.rsqrt(ms + RMS_EPS) * g


def _params(sem):
    return pltpu.CompilerParams(dimension_semantics=sem, vmem_limit_bytes=VMEM_LIMIT)


def _ada_kernel(c_ref, w_ref, b_ref, o_ref):
    c = c_ref[...]
    cond = c * _sigmoid(c)
    o_ref[...] = jnp.dot(cond, w_ref[...], preferred_element_type=F32,
                         precision=lax.Precision.HIGHEST) + b_ref[...]


def _ada(c, w, b):
    bsz = c.shape[0]
    n_out = w.shape[1]
    return pl.pallas_call(
        _ada_kernel,
        grid=(n_out // D_MODEL,),
        in_specs=[pl.BlockSpec((bsz, D_MODEL), lambda j: (0, 0)),
                  pl.BlockSpec((D_MODEL, D_MODEL), lambda j: (0, j)),
                  pl.BlockSpec((1, D_MODEL), lambda j: (0, j))],
        out_specs=pl.BlockSpec((bsz, D_MODEL), lambda j: (0, j)),
        out_shape=jax.ShapeDtypeStruct((bsz, n_out), F32),
        compiler_params=_params(("arbitrary",)),
        name="ada_mod",
    )(c, w, b.reshape(1, n_out))


def _inproj_kernel(x_ref, mod_ref, g_ref, w_ref, xl_ref, yl_ref, q_ref, k_ref, v_ref, f_ref):
    x = x_ref[...]
    m = mod_ref[0]
    sh = m[:, 0:D_MODEL]
    sc = m[:, D_MODEL:2 * D_MODEL]
    h = _rms(x, g_ref[...]) * (1.0 + sc) + sh
    hb = h.astype(BF16)

    def proj(lo, hi):
        return jnp.dot(hb, w_ref[:, lo:hi], preferred_element_type=F32)

    xl_ref[...] = proj(0, D_LRU)
    yl_ref[...] = proj(D_LRU, 2 * D_LRU)
    o = 2 * D_LRU
    q_ref[...] = (proj(o, o + D_ATTN) * (HEAD_DIM ** -0.5)).astype(BF16)
    k_ref[...] = proj(o + D_ATTN, o + 2 * D_ATTN).astype(BF16)
    v_ref[...] = proj(o + 2 * D_ATTN, o + 3 * D_ATTN).astype(BF16)
    f_ref[...] = proj(o + 3 * D_ATTN, o + 3 * D_ATTN + LANES)


def _inproj(x2d, mod3, g, w_in_b, seq):
    n = x2d.shape[0]
    tpb = seq // TOKEN_TILE
    wcols = w_in_b.shape[1]
    row = lambda i: (i, 0)
    return pl.pallas_call(
        _inproj_kernel,
        grid=(n // TOKEN_TILE,),
        in_specs=[pl.BlockSpec((TOKEN_TILE, D_MODEL), row),
                  pl.BlockSpec((1, 1, N_MOD * D_MODEL), lambda i: (i // tpb, 0, 0)),
                  pl.BlockSpec((1, D_MODEL), lambda i: (0, 0)),
                  pl.BlockSpec((D_MODEL, wcols), lambda i: (0, 0))],
        out_specs=[pl.BlockSpec((TOKEN_TILE, D_LRU), row),
                   pl.BlockSpec((TOKEN_TILE, D_LRU), row),
                   pl.BlockSpec((TOKEN_TILE, D_ATTN), row),
                   pl.BlockSpec((TOKEN_TILE, D_ATTN), row),
                   pl.BlockSpec((TOKEN_TILE, D_ATTN), row),
                   pl.BlockSpec((TOKEN_TILE, LANES), row)],
        out_shape=[jax.ShapeDtypeStruct((n, D_LRU), F32),
                   jax.ShapeDtypeStruct((n, D_LRU), F32),
                   jax.ShapeDtypeStruct((n, D_ATTN), BF16),
                   jax.ShapeDtypeStruct((n, D_ATTN), BF16),
                   jax.ShapeDtypeStruct((n, D_ATTN), BF16),
                   jax.ShapeDtypeStruct((n, LANES), F32)],
        compiler_params=_params(("arbitrary",)),
        name="in_proj",
    )(x2d, mod3, g, w_in_b)


def _shift_rows(x, d, fill):
    rows = lax.broadcasted_iota(jnp.int32, x.shape, 0)
    return jnp.where(rows >= d, pltpu.roll(x, d, 0), fill)


def _lru_kernel(xl_ref, yl_ref, f_ref, cw_ref, cb_ref, wg_ref, bg_ref, lam_ref, fb_ref, gn_ref,
                o_ref, ccol_ref, crow_ref, prev_ref, hcar_ref, ccar_ref):
    j = pl.program_id(1)

    @pl.when(j == 0)
    def _():
        prev_ref[...] = jnp.zeros_like(prev_ref)
        hcar_ref[...] = jnp.zeros_like(hcar_ref)
        ccar_ref[...] = jnp.zeros_like(ccar_ref)

    x = xl_ref[...]
    t = x.shape[0]
    prev = prev_ref[...]
    rows8 = lax.broadcasted_iota(jnp.int32, prev.shape, 0)
    cw = cw_ref[...]
    xc = x * cw[CONV_WIDTH - 1:CONV_WIDTH, :] + cb_ref[...]
    for d in range(1, CONV_WIDTH):
        rolled = pltpu.roll(x, d, 0)
        head = jnp.where(rows8 < d, pltpu.roll(prev, d, 0), rolled[:SUBLANES])
        shifted = jnp.concatenate([head, rolled[SUBLANES:]], axis=0)
        xc = xc + shifted * cw[CONV_WIDTH - 1 - d:CONV_WIDTH - d, :]
    prev_ref[...] = x[t - SUBLANES:]

    gates = jnp.dot(xc.astype(BF16), wg_ref[...], preferred_element_type=F32) + bg_ref[...]
    r = _sigmoid(gates[:, :D_LRU])
    i = _sigmoid(gates[:, D_LRU:])
    nlam = -lam_ref[...]
    softplus = jnp.maximum(nlam, 0.0) + jnp.log1p(jnp.exp(-jnp.abs(nlam)))
    log_a = (-LRU_C) * r * softplus
    a = jnp.exp(log_a)
    b = jnp.sqrt(1.0 - jnp.exp(2.0 * log_a)) * (i * xc)

    d = 1
    while d < t:
        b = a * _shift_rows(b, d, 0.0) + b
        a = a * _shift_rows(a, d, 1.0)
        d *= 2
    h = b + a * hcar_ref[SUBLANES - 1:SUBLANES, :]
    hcar_ref[...] = h[t - SUBLANES:]

    y = yl_ref[...]
    gelu = 0.5 * y * (1.0 + jnp.tanh(0.7978845608028654 * (y + 0.044715 * (y * y * y))))
    o_ref[...] = _rms(h * gelu, gn_ref[...]).astype(o_ref.dtype)

    z = f_ref[...] + fb_ref[...]
    c = jnp.minimum(z, 0.0) - jnp.log1p(jnp.exp(-jnp.abs(z)))
    d = 1
    while d < t:
        c = c + _shift_rows(c, d, 0.0)
        d *= 2
    c = c + ccar_ref[SUBLANES - 1:SUBLANES, :]
    ccar_ref[...] = c[t - SUBLANES:]
    ccol_ref[...] = c
    crow_ref[0] = c.T[:SUBLANES, :]


def _lru(xl, yl, f, cw, cb, wg, bg, lam, fb, gn, bsz, seq):
    n = xl.shape[0]
    nt = seq // SCAN_TILE
    row = lambda b, j: (b * nt + j, 0)
    const = lambda b, j: (0, 0)
    return pl.pallas_call(
        _lru_kernel,
        grid=(bsz, nt),
        in_specs=[pl.BlockSpec((SCAN_TILE, D_LRU), row),
                  pl.BlockSpec((SCAN_TILE, D_LRU), row),
                  pl.BlockSpec((SCAN_TILE, LANES), row),
                  pl.BlockSpec((CONV_WIDTH, D_LRU), const),
                  pl.BlockSpec((1, D_LRU), const),
                  pl.BlockSpec((D_LRU, 2 * D_LRU), const),
                  pl.BlockSpec((1, 2 * D_LRU), const),
                  pl.BlockSpec((1, D_LRU), const),
                  pl.BlockSpec((1, LANES), const),
                  pl.BlockSpec((1, D_LRU), const)],
        out_specs=[pl.BlockSpec((SCAN_TILE, D_LRU), row),
                   pl.BlockSpec((SCAN_TILE, LANES), row),
                   pl.BlockSpec((1, SUBLANES, SCAN_TILE), lambda b, j: (b, 0, j))],
        out_shape=[jax.ShapeDtypeStruct((n, D_LRU), BF16),
                   jax.ShapeDtypeStruct((n, LANES), F32),
                   jax.ShapeDtypeStruct((bsz, SUBLANES, seq), F32)],
        scratch_shapes=[pltpu.VMEM((SUBLANES, D_LRU), F32),
                        pltpu.VMEM((SUBLANES, D_LRU), F32),
                        pltpu.VMEM((SUBLANES, LANES), F32)],
        compiler_params=_params(("arbitrary", "arbitrary")),
        name="rg_lru",
    )(xl, yl, f, cw, cb, wg, bg, lam, fb, gn)


def _attn_kernel(q_ref, k_ref, v_ref, ccol_ref, crow_ref, o_ref):
    hg = pl.program_id(1)
    qi = pl.program_id(2)
    tq = q_ref.shape[0]
    q2 = q_ref[...]
    lane = lax.broadcasted_iota(jnp.int32, q2.shape, 1)
    ccol = ccol_ref[...]
    rows = lax.broadcasted_iota(jnp.int32, (tq, tq), 0)
    cols = lax.broadcasted_iota(jnp.int32, (tq, tq), 1)
    causal = cols <= rows
    nt_dims = (((1,), (1,)), ((), ()))

    outs = []
    for half in range(2):
        head = 2 * hg + half
        in_half = (lane >= half * HEAD_DIM) & (lane < (half + 1) * HEAD_DIM)
        qh = jnp.where(in_half, q2, jnp.zeros_like(q2))
        cq = jnp.sum(jnp.where(lane == head, ccol, 0.0), axis=-1, keepdims=True)

        def step(jk, carry, masked):
            m, l, acc = carry
            start = pl.multiple_of(jk * tq, tq)
            k2 = k_ref[pl.ds(start, tq), :]
            v2 = v_ref[pl.ds(start, tq), :]
            ck = crow_ref[0, pl.ds(head, 1), pl.ds(start, tq)]
            s = lax.dot_general(qh, k2, nt_dims, preferred_element_type=F32) - ck
            if masked:
                s = jnp.where(causal, s, NEG_BIG)
            m_new = jnp.maximum(m, jnp.max(s, axis=-1, keepdims=True) + cq)
            p = jnp.exp(s - (m_new - cq))
            alpha = jnp.exp(m - m_new)
            l_new = alpha * l + jnp.sum(p, axis=-1, keepdims=True)
            acc_new = alpha * acc + jnp.dot(p.astype(BF16), v2, preferred_element_type=F32)
            return m_new, l_new, acc_new

        init = (jnp.full((tq, 1), NEG_BIG, F32), jnp.zeros((tq, 1), F32),
                jnp.zeros((tq, LANES), F32))
        carry = lax.fori_loop(0, qi, functools.partial(step, masked=False), init)
        m, l, acc = step(qi, carry, masked=True)
        outs.append(acc / l)

    o_ref[...] = jnp.where(lane < HEAD_DIM, outs[0], outs[1])


def _attention(q, k, v, ccol, crow, bsz, seq):
    n = q.shape[0]
    nq = seq // ATTN_TILE
    qmap = lambda b, g, i: (b * nq + i, g)
    kvmap = lambda b, g, i: (b, g)
    return pl.pallas_call(
        _attn_kernel,
        grid=(bsz, D_ATTN // LANES, nq),
        in_specs=[pl.BlockSpec((ATTN_TILE, LANES), qmap),
                  pl.BlockSpec((seq, LANES), kvmap),
                  pl.BlockSpec((seq, LANES), kvmap),
                  pl.BlockSpec((ATTN_TILE, LANES), lambda b, g, i: (b * nq + i, 0)),
                  pl.BlockSpec((1, SUBLANES, seq), lambda b, g, i: (b, 0, 0))],
        out_specs=pl.BlockSpec((ATTN_TILE, LANES), qmap),
        out_shape=jax.ShapeDtypeStruct((n, D_ATTN), F32),
        compiler_params=_params(("arbitrary", "arbitrary", "arbitrary")),
        name="fox_attention",
    )(q, k, v, ccol, crow)


def _outproj_kernel(lru_ref, attn_ref, x_ref, mod_ref, gna_ref, wo_ref, gpost_ref, gpre_ref,
                    wr_ref, br_ref, x1_ref, h2_ref, idx_ref, tw_ref):
    m = mod_ref[0]
    g_m = m[:, 2 * D_MODEL:3 * D_MODEL]
    sh_f = m[:, 3 * D_MODEL:4 * D_MODEL]
    sc_f = m[:, 4 * D_MODEL:5 * D_MODEL]
    attn_n = _rms(attn_ref[...], gna_ref[...]).astype(BF16)
    y = (jnp.dot(lru_ref[...], wo_ref[0:D_LRU, :], preferred_element_type=F32)
         + jnp.dot(attn_n, wo_ref[D_LRU:, :], preferred_element_type=F32))
    x1 = x_ref[...] + g_m * _rms(y, gpost_ref[...])
    x1_ref[...] = x1
    h2 = _rms(x1, gpre_ref[...]) * (1.0 + sc_f) + sh_f
    for cc in range(ROW_CHUNKS):
        h2_ref[pl.ds(cc, h2.shape[0], stride=ROW_CHUNKS), :] = h2[:, cc * LANES:(cc + 1) * LANES]

    logits = jnp.dot(h2, wr_ref[...], preferred_element_type=F32,
                     precision=lax.Precision.HIGHEST) + br_ref[...]
    lane = lax.broadcasted_iota(jnp.int32, logits.shape, 1)
    lane_f = lane.astype(F32)
    cur = logits
    vals, idxs = [], []
    for _ in range(TOP_K):
        mx = jnp.max(cur, axis=-1, keepdims=True)
        ix = jnp.min(jnp.where(cur == mx, lane_f, float(LANES)), axis=-1, keepdims=True)
        vals.append(mx)
        idxs.append(ix)
        cur = jnp.where(lane_f == ix, -jnp.inf, cur)
    exps = [jnp.exp(vv - vals[0]) for vv in vals]
    den = exps[0] + exps[1] + exps[2] + exps[3]
    idx_out = jnp.zeros(logits.shape, F32)
    w_out = jnp.zeros(logits.shape, F32)
    for kk in range(TOP_K):
        idx_out = jnp.where(lane == kk, idxs[kk], idx_out)
        w_out = jnp.where(lane == kk, exps[kk] / den, w_out)
    idx_ref[...] = idx_out.astype(jnp.int32)
    tw_ref[...] = w_out


def _outproj(lru_n, attn, x2d, mod3, gna, wo_b, gpost, gpre, wr, br, seq):
    n = x2d.shape[0]
    tpb = seq // TOKEN_TILE
    row = lambda i: (i, 0)
    const = lambda i: (0, 0)
    return pl.pallas_call(
        _outproj_kernel,
        grid=(n // TOKEN_TILE,),
        in_specs=[pl.BlockSpec((TOKEN_TILE, D_LRU), row),
                  pl.BlockSpec((TOKEN_TILE, D_ATTN), row),
                  pl.BlockSpec((TOKEN_TILE, D_MODEL), row),
                  pl.BlockSpec((1, 1, N_MOD * D_MODEL), lambda i: (i // tpb, 0, 0)),
                  pl.BlockSpec((1, D_ATTN), const),
                  pl.BlockSpec((D_MODEL, D_MODEL), const),
                  pl.BlockSpec((1, D_MODEL), const),
                  pl.BlockSpec((1, D_MODEL), const),
                  pl.BlockSpec((D_MODEL, LANES), const),
                  pl.BlockSpec((1, LANES), const)],
        out_specs=[pl.BlockSpec((TOKEN_TILE, D_MODEL), row),
                   pl.BlockSpec((TOKEN_TILE * ROW_CHUNKS, LANES), row),
                   pl.BlockSpec((TOKEN_TILE, LANES), row),
                   pl.BlockSpec((TOKEN_TILE, LANES), row)],
        out_shape=[jax.ShapeDtypeStruct((n, D_MODEL), F32),
                   jax.ShapeDtypeStruct((n * ROW_CHUNKS, LANES), F32),
                   jax.ShapeDtypeStruct((n, LANES), jnp.int32),
                   jax.ShapeDtypeStruct((n, LANES), F32)],
        compiler_params=_params(("arbitrary",)),
        name="out_proj_router",
    )(lru_n, attn, x2d, mod3, gna, wo_b, gpost, gpre, wr, br)


def _rank_kernel(idx_ref, rank_ref, cnt_ref, base_ref):
    i = pl.program_id(0)

    @pl.when(i == 0)
    def _():
        base_ref[...] = jnp.zeros_like(base_ref)

    idx = idx_ref[...]
    t = idx.shape[0]
    lane = lax.broadcasted_iota(jnp.int32, idx.shape, 1)
    onehots = [lane == idx[:, kk:kk + 1] for kk in range(TOP_K)]
    tot = jnp.zeros(idx.shape, F32)
    for oh in onehots:
        tot = tot + jnp.where(oh, 1.0, 0.0)
    rows = lax.broadcasted_iota(jnp.int32, (t, t), 0)
    cols = lax.broadcasted_iota(jnp.int32, (t, t), 1)
    lower = jnp.where(cols < rows, 1.0, 0.0).astype(BF16)
    before = jnp.dot(lower, tot.astype(BF16), preferred_element_type=F32)
    base = base_ref[0:1, :]
    seen = before + base
    rank = jnp.zeros(idx.shape, F32)
    for kk, oh in enumerate(onehots):
        rk = jnp.sum(jnp.where(oh, seen, 0.0), axis=-1, keepdims=True)
        rank = jnp.where(lane == kk, rk, rank)
    rank_ref[...] = rank.astype(jnp.int32)
    new_base = base + jnp.sum(tot, axis=0, keepdims=True)
    base_ref[...] = jnp.broadcast_to(new_base, base_ref.shape)
    cnt_ref[...] = jnp.broadcast_to(new_base, cnt_ref.shape).astype(jnp.int32)


def _ranks(idx):
    n = idx.shape[0]
    return pl.pallas_call(
        _rank_kernel,
        grid=(n // ROUTE_TILE,),
        in_specs=[pl.BlockSpec((ROUTE_TILE, LANES), lambda i: (i, 0))],
        out_specs=[pl.BlockSpec((ROUTE_TILE, LANES), lambda i: (i, 0)),
                   pl.BlockSpec((SUBLANES, LANES), lambda i: (0, 0))],
        out_shape=[jax.ShapeDtypeStruct((n, LANES), jnp.int32),
                   jax.ShapeDtypeStruct((SUBLANES, LANES), jnp.int32)],
        scratch_shapes=[pltpu.VMEM((SUBLANES, LANES), F32)],
        compiler_params=_params(("arbitrary",)),
        name="route_ranks",
    )(idx)


def _expert_kernel(bexp_ref, nvalid_ref, nexp_ref, nused_ref, inv_ref,
                   h2_hbm, wup_hbm, bup_ref, wdn_hbm, bdn_ref, out_hbm,
                   xbuf, ybuf, xb_buf, act_buf, wup_f, wdn_f, wup_b, wdn_b, gsem, ssem, wsem,
                   *, n_tokens, n_blocks):
    b = pl.program_id(0)
    nused = nused_ref[0]
    tile_rows = ROW_BLOCK * ROW_CHUNKS

    def weight_copies(e):
        return (pltpu.make_async_copy(wup_hbm.at[e], wup_f, wsem.at[0]),
                pltpu.make_async_copy(wdn_hbm.at[e], wdn_f, wsem.at[1]))

    def tile_of(row):
        start = row * ROW_CHUNKS
        if not isinstance(row, int):
            start = pl.multiple_of(start, ROW_CHUNKS)
        return pl.ds(start, ROW_CHUNKS)

    def gather_copy(blk, slot, r):
        pair = inv_ref[blk * ROW_BLOCK + r]
        tok = lax.rem(pair, n_tokens)
        return pltpu.make_async_copy(h2_hbm.at[tile_of(tok), :], xbuf.at[slot, tile_of(r), :],
                                     gsem.at[slot])

    def scatter_copy(blk, slot, r):
        pair = inv_ref[blk * ROW_BLOCK + r]
        return pltpu.make_async_copy(ybuf.at[slot, tile_of(r), :], out_hbm.at[tile_of(pair), :],
                                     ssem.at[slot])

    def wait_gather(slot):
        pltpu.make_async_copy(h2_hbm.at[pl.ds(0, tile_rows), :], xbuf.at[slot], gsem.at[slot]).wait()

    def wait_scatter(slot, nrows):
        @pl.when(nrows > 0)
        def _():
            nr = pl.multiple_of(nrows * ROW_CHUNKS, ROW_CHUNKS)
            pltpu.make_async_copy(ybuf.at[slot, pl.ds(0, nr), :], out_hbm.at[pl.ds(0, nr), :],
                                  ssem.at[slot]).wait()

    @pl.when(b < nused)
    def _():
        e = bexp_ref[b]
        slot = lax.rem(b, 2)
        nslot = 1 - slot
        prev_blk = jnp.maximum(b - 1, 0)
        next_blk = jnp.minimum(b + 1, n_blocks - 1)
        first_of_expert = jnp.logical_or(b == 0, e != bexp_ref[prev_blk])

        @pl.when(b == 0)
        def _():
            for cp in weight_copies(e):
                cp.start()

            def start_row(r, c):
                gather_copy(0, 0, r).start()
                return c
            lax.fori_loop(0, ROW_BLOCK, start_row, 0)

        @pl.when(first_of_expert)
        def _():
            for cp in weight_copies(e):
                cp.wait()
            wup_b[...] = wup_f[...].astype(BF16)
            wdn_b[...] = wdn_f[...].astype(BF16)
            nxt = nexp_ref[b]

            @pl.when(nxt >= 0)
            def _():
                for cp in weight_copies(nxt):
                    cp.start()

        wait_gather(slot)

        @pl.when(b >= 2)
        def _():
            wait_scatter(slot, nvalid_ref[jnp.maximum(b - 2, 0)])

        n_prev = jnp.where(b >= 1, nvalid_ref[prev_blk], 0)

        def start_row_copies(lo, hi):
            for r in range(lo, hi):
                gather_copy(next_blk, nslot, r).start()

                @pl.when(r < n_prev)
                def _():
                    scatter_copy(prev_blk, nslot, r).start()

        n_tiles_e = D_EXPERT // MXU_TILE
        n_tiles_m = D_MODEL // MXU_TILE
        n_stages = 2 * n_tiles_e + n_tiles_m
        per_stage = -(-ROW_BLOCK // n_stages)
        stage = [0]

        def next_stage_copies():
            lo = min(stage[0] * per_stage, ROW_BLOCK)
            hi = min(lo + per_stage, ROW_BLOCK)
            stage[0] += 1
            start_row_copies(lo, hi)

        for cc in range(ROW_CHUNKS):
            xb_buf[:, cc * LANES:(cc + 1) * LANES] = (
                xbuf[slot, pl.ds(cc, ROW_BLOCK, stride=ROW_CHUNKS), :].astype(BF16))

        for j in range(n_tiles_e):
            lo, hi = j * MXU_TILE, (j + 1) * MXU_TILE
            next_stage_copies()
            glu = jnp.dot(xb_buf[...], wup_b[:, lo:hi], preferred_element_type=F32) + bup_ref[0, :, lo:hi]
            next_stage_copies()
            lin = (jnp.dot(xb_buf[...], wup_b[:, D_EXPERT + lo:D_EXPERT + hi], preferred_element_type=F32)
                   + bup_ref[0, :, D_EXPERT + lo:D_EXPERT + hi])
            glu = jnp.minimum(glu, SWIGLU_LIMIT)
            lin = jnp.clip(lin, -SWIGLU_LIMIT, SWIGLU_LIMIT)
            act = glu * _sigmoid(SWIGLU_ALPHA * glu) * (lin + 1.0)
            act_buf[:, lo:hi] = act.astype(BF16)

        for j in range(n_tiles_m):
            lo, hi = j * MXU_TILE, (j + 1) * MXU_TILE
            next_stage_copies()
            y = jnp.dot(act_buf[...], wdn_b[:, lo:hi], preferred_element_type=F32) + bdn_ref[0, :, lo:hi]
            for cc in range(MXU_TILE // LANES):
                ybuf[slot, pl.ds(lo // LANES + cc, ROW_BLOCK, stride=ROW_CHUNKS), :] = (
                    y[:, cc * LANES:(cc + 1) * LANES])

        @pl.when(b == nused - 1)
        def _():
            n_cur = nvalid_ref[b]

            def start_row(r, c):
                scatter_copy(b, slot, r).start()
                return c
            lax.fori_loop(0, n_cur, start_row, 0)
            wait_scatter(nslot, n_prev)
            wait_scatter(slot, n_cur)
            wait_gather(nslot)


def _experts(bexp, nvalid, nexp, nused, inv, h2t, w_up, b_up, w_down, b_down):
    n = h2t.shape[0] // ROW_CHUNKS
    n_blocks = bexp.shape[0]
    tile_rows = ROW_BLOCK * ROW_CHUNKS
    bias_map = lambda b, be, nv, nx, nu, iv: (be[b], 0, 0)
    grid_spec = pltpu.PrefetchScalarGridSpec(
        num_scalar_prefetch=5,
        grid=(n_blocks,),
        in_specs=[pl.BlockSpec(memory_space=pl.ANY),
                  pl.BlockSpec(memory_space=pl.ANY),
                  pl.BlockSpec((1, 1, 2 * D_EXPERT), bias_map),
                  pl.BlockSpec(memory_space=pl.ANY),
                  pl.BlockSpec((1, 1, D_MODEL), bias_map)],
        out_specs=pl.BlockSpec(memory_space=pl.ANY),
        scratch_shapes=[pltpu.VMEM((2, tile_rows, LANES), F32),
                        pltpu.VMEM((2, tile_rows, LANES), F32),
                        pltpu.VMEM((ROW_BLOCK, D_MODEL), BF16),
                        pltpu.VMEM((ROW_BLOCK, D_EXPERT), BF16),
                        pltpu.VMEM((D_MODEL, 2 * D_EXPERT), F32),
                        pltpu.VMEM((D_EXPERT, D_MODEL), F32),
                        pltpu.VMEM((D_MODEL, 2 * D_EXPERT), BF16),
                        pltpu.VMEM((D_EXPERT, D_MODEL), BF16),
                        pltpu.SemaphoreType.DMA((2,)),
                        pltpu.SemaphoreType.DMA((2,)),
                        pltpu.SemaphoreType.DMA((2,))],
    )
    return pl.pallas_call(
        functools.partial(_expert_kernel, n_tokens=n, n_blocks=n_blocks),
        grid_spec=grid_spec,
        out_shape=jax.ShapeDtypeStruct((n * TOP_K * ROW_CHUNKS, LANES), F32),
        compiler_params=_params(("arbitrary",)),
        name="moe_experts",
    )(bexp, nvalid, nexp, nused, inv, h2t, w_up, b_up.reshape(N_EXPERTS, 1, 2 * D_EXPERT),
      w_down, b_down.reshape(N_EXPERTS, 1, D_MODEL))


def _combine_kernel(r0_ref, r1_ref, r2_ref, r3_ref, tw_ref, x1_ref, mod_ref, g_ref, o_ref):
    tw = tw_ref[...]
    t = tw.shape[0]
    r_refs = (r0_ref, r1_ref, r2_ref, r3_ref)
    chunks = []
    sumsq = jnp.zeros((t, 1), F32)
    for cc in range(ROW_CHUNKS):
        rows_cc = pl.ds(cc, t, stride=ROW_CHUNKS)
        y = r_refs[0][rows_cc, :] * tw[:, 0:1]
        for kk in range(1, TOP_K):
            y = y + r_refs[kk][rows_cc, :] * tw[:, kk:kk + 1]
        chunks.append(y)
        sumsq = sumsq + jnp.sum(y * y, axis=-1, keepdims=True)
    inv_rms = lax.rsqrt(sumsq * (1.0 / D_MODEL) + RMS_EPS)
    m = mod_ref[0]
    for cc, y in enumerate(chunks):
        lo, hi = cc * LANES, (cc + 1) * LANES
        g_f = m[:, 5 * D_MODEL + lo:5 * D_MODEL + hi]
        o_ref[:, lo:hi] = x1_ref[:, lo:hi] + g_f * (y * inv_rms * g_ref[:, lo:hi])


def _combine(rows, tw, x1, mod3, g, seq):
    n = x1.shape[0]
    tpb = seq // COMBINE_TILE
    nt = n // COMBINE_TILE
    row = lambda i: (i, 0)
    slot_specs = [pl.BlockSpec((COMBINE_TILE * ROW_CHUNKS, LANES),
                               functools.partial(lambda i, kk: (kk * nt + i, 0), kk=kk))
                  for kk in range(TOP_K)]
    return pl.pallas_call(
        _combine_kernel,
        grid=(nt,),
        in_specs=slot_specs + [
            pl.BlockSpec((COMBINE_TILE, LANES), row),
            pl.BlockSpec((COMBINE_TILE, D_MODEL), row),
            pl.BlockSpec((1, 1, N_MOD * D_MODEL), lambda i: (i // tpb, 0, 0)),
            pl.BlockSpec((1, D_MODEL), lambda i: (0, 0))],
        out_specs=pl.BlockSpec((COMBINE_TILE, D_MODEL), row),
        out_shape=jax.ShapeDtypeStruct((n, D_MODEL), F32),
        compiler_params=_params(("arbitrary",)),
        name="moe_combine",
    )(rows, rows, rows, rows, tw, x1, mod3, g)


def _block_diag(w):
    g, d, _ = w.shape
    eye = jnp.eye(g, dtype=w.dtype)
    return jnp.einsum("gij,gh->gihj", w, eye).reshape(g * d, g * d)


def _layer(x2d, mod, bsz, seq, norm_mix_pre, norm_mix_post, w_in, conv_w, conv_b, lru_wa, lru_ba,
           lru_wx, lru_bx, lru_lambda, attn_fb, gn_lru, gn_attn, w_out, norm_ffn_pre, norm_ffn_post,
           w_router, b_router, w_up, b_up, w_down, b_down):
    n = x2d.shape[0]
    mod3 = mod.reshape(bsz, 1, N_MOD * D_MODEL)
    d_in = w_in.shape[1]
    w_in_b = jnp.pad(w_in, ((0, 0), (0, 2 * D_LRU + 3 * D_ATTN + LANES - d_in))).astype(BF16)
    xl, yl, q, k, v, f = _inproj(x2d, mod3, norm_mix_pre.reshape(1, -1), w_in_b, seq)

    wg = jnp.concatenate([_block_diag(lru_wa), _block_diag(lru_wx)], axis=1).astype(BF16)
    bg = jnp.concatenate([lru_ba.reshape(1, -1), lru_bx.reshape(1, -1)], axis=1)
    fb = jnp.pad(attn_fb.reshape(1, -1), ((0, 0), (0, LANES - ATTN_HEADS)))
    lru_n, ccol, crow = _lru(xl, yl, f, conv_w, conv_b.reshape(1, -1), wg, bg,
                             lru_lambda.reshape(1, -1), fb, gn_lru.reshape(1, -1), bsz, seq)

    attn = _attention(q, k, v, ccol, crow, bsz, seq)

    wr = jnp.pad(w_router, ((0, 0), (0, LANES - N_EXPERTS)))
    br = jnp.pad(b_router.reshape(1, -1), ((0, 0), (0, LANES - N_EXPERTS)), constant_values=NEG_BIG)
    x1, h2, idx, tw = _outproj(lru_n, attn, x2d, mod3, gn_attn.reshape(1, -1), w_out.astype(BF16),
                               norm_mix_post.reshape(1, -1), norm_ffn_pre.reshape(1, -1), wr, br, seq)

    rank, cnt = _ranks(idx)
    counts = cnt[0, :N_EXPERTS]
    padded = (counts + ROW_BLOCK - 1) // ROW_BLOCK * ROW_BLOCK
    pends = jnp.cumsum(padded)
    pstarts = pends - padded
    n_blocks = n * TOP_K // ROW_BLOCK + N_EXPERTS
    n_rows = n_blocks * ROW_BLOCK
    blk_start = jnp.arange(n_blocks, dtype=jnp.int32) * ROW_BLOCK
    eids = jnp.arange(N_EXPERTS, dtype=jnp.int32)
    bexp = jnp.minimum(jnp.sum((pends[None, :] <= blk_start[:, None]).astype(jnp.int32), axis=1),
                       N_EXPERTS - 1)
    onehot_b = bexp[:, None] == eids[None, :]
    blk_count = jnp.sum(jnp.where(onehot_b, counts[None, :], 0), axis=1)
    blk_pstart = jnp.sum(jnp.where(onehot_b, pstarts[None, :], 0), axis=1)
    nused = (pends[-1] // ROW_BLOCK).astype(jnp.int32)
    nvalid = jnp.clip(blk_count - (blk_start - blk_pstart), 0, ROW_BLOCK)
    nvalid = jnp.where(jnp.arange(n_blocks) < nused, nvalid, 0).astype(jnp.int32)
    later_used = (counts[None, :] > 0) & (eids[None, :] > eids[:, None])
    next_used = jnp.min(jnp.where(later_used, eids[None, :], N_EXPERTS), axis=1)
    next_used = jnp.where(next_used == N_EXPERTS, -1, next_used)
    nexp = jnp.sum(jnp.where(onehot_b, next_used[None, :], 0), axis=1).astype(jnp.int32)

    e_flat = idx[:, :TOP_K].T.reshape(-1)
    pair_pstart = jnp.sum(jnp.where(e_flat[:, None] == eids[None, :], pstarts[None, :], 0), axis=1)
    dest = pair_pstart + rank[:, :TOP_K].T.reshape(-1)
    spare = n * TOP_K + jnp.arange(n_rows, dtype=jnp.int32) % ROW_BLOCK
    inv = spare.at[dest].set(jnp.arange(n * TOP_K, dtype=jnp.int32))

    rows = _experts(bexp, nvalid, nexp, nused.reshape(1), inv, h2, w_up, b_up, w_down, b_down)
    return _combine(rows, tw, x1, mod3, norm_ffn_post.reshape(1, -1), seq)


def kernel(x, c, w_ada, b_ada, norm_mix_pre, norm_mix_post, w_in, conv_w, conv_b, lru_wa, lru_ba,
           lru_wx, lru_bx, lru_lambda, attn_fb, gn_lru, gn_attn, w_out, norm_ffn_pre, norm_ffn_post,
           w_router, b_router, w_up, b_up, w_down, b_down):
    bsz, seq, d = x.shape
    x2d = x.reshape(bsz * seq, d)
    for l in range(w_ada.shape[0]):
        mod = _ada(c, w_ada[l], b_ada[l])
        x2d = _layer(x2d, mod, bsz, seq, norm_mix_pre[l], norm_mix_post[l], w_in[l], conv_w[l],
                     conv_b[l], lru_wa[l], lru_ba[l], lru_wx[l], lru_bx[l], lru_lambda[l],
                     attn_fb[l], gn_lru[l], gn_attn[l], w_out[l], norm_ffn_pre[l], norm_ffn_post[l],
                     w_router[l], b_router[l], w_up[l], b_up[l], w_down[l], b_down[l])
    return x2d.reshape(bsz, seq, d)
```

```python
import functools

import jax
import jax.numpy as jnp
from jax import lax
from jax.experimental import pallas as pl
from jax.experimental.pallas import tpu as pltpu

F32 = jnp.float32
BF16 = jnp.bfloat16

D_MODEL = 1024
D_LRU = 512
LRU_BLOCKS = 8
CONV_WIDTH = 4
LRU_C = 8.0
D_ATTN = 512
ATTN_HEADS = 8
HEAD_DIM = 64
N_EXPERTS = 32
TOP_K = 4
D_EXPERT = 1024
SWIGLU_ALPHA = 1.702
SWIGLU_LIMIT = 7.0
N_MOD = 6
RMS_EPS = 1e-6

LANES = 128
SUBLANES = 8
MXU_TILE = 256
VMEM_LIMIT = 48 * 1024 * 1024

TOKEN_TILE = 512
SCAN_TILE = 256
ATTN_TILE = 256
ATTN_KEY_TILE = 256
ATTN_LOOKAHEAD = 4
ROUTE_TILE = 512
ROW_BLOCK = 256
GATHER_AHEAD = 2
GATHER_SLOTS = GATHER_AHEAD + 1
INVERT_UNROLL = 8
SCATTER_DMA_PRIORITY = 1
COMBINE_TILE = 256
NEG_BIG = -1e30
LOG2_E = 1.4426950408889634
ROW_CHUNKS = D_MODEL // LANES
assert ROW_CHUNKS == SUBLANES


def _sigmoid(z):
    return 1.0 / (1.0 + jnp.exp(-z))


def _rms(x, g):
    ms = jnp.mean(x * x, axis=-1, keepdims=True)
    return x * lax.rsqrt(ms + RMS_EPS) * g


def _params(sem):
    return pltpu.CompilerParams(dimension_semantics=sem, vmem_limit_bytes=VMEM_LIMIT)


def _ada_kernel(c_ref, w_ref, b_ref, o_ref):
    c = c_ref[...]
    cond = c * _sigmoid(c)
    o_ref[...] = jnp.dot(cond, w_ref[...], preferred_element_type=F32,
                         precision=lax.Precision.HIGHEST) + b_ref[...]


def _ada(c, w, b):
    bsz = c.shape[0]
    n_out = w.shape[1]
    return pl.pallas_call(
        _ada_kernel,
        grid=(n_out // D_MODEL,),
        in_specs=[pl.BlockSpec((bsz, D_MODEL), lambda j: (0, 0)),
                  pl.BlockSpec((D_MODEL, D_MODEL), lambda j: (0, j)),
                  pl.BlockSpec((1, D_MODEL), lambda j: (0, j))],
        out_specs=pl.BlockSpec((bsz, D_MODEL), lambda j: (0, j)),
        out_shape=jax.ShapeDtypeStruct((bsz, n_out), F32),
        compiler_params=_params(("arbitrary",)),
        name="ada_mod",
    )(c, w, b.reshape(1, n_out))


def _inproj_kernel(x_ref, mod_ref, g_ref, w_ref, xl_ref, yl_ref, q_ref, k_ref, vt_ref, f_ref):
    x = x_ref[...]
    m = mod_ref[0]
    sh = m[:, 0:D_MODEL]
    sc = m[:, D_MODEL:2 * D_MODEL]
    h = _rms(x, g_ref[...]) * (1.0 + sc) + sh
    hb = h.astype(BF16)

    def proj(lo, hi):
        return jnp.dot(hb, w_ref[:, lo:hi], preferred_element_type=F32)

    xl_ref[...] = proj(0, D_LRU)
    yl_ref[...] = proj(D_LRU, 2 * D_LRU)
    o = 2 * D_LRU
    q_ref[...] = (proj(o, o + D_ATTN) * (LOG2_E * HEAD_DIM ** -0.5)).astype(BF16)
    k_ref[...] = proj(o + D_ATTN, o + 2 * D_ATTN).astype(BF16)
    vt_ref[...] = proj(o + 2 * D_ATTN, o + 3 * D_ATTN).T.astype(BF16)
    f_ref[...] = proj(o + 3 * D_ATTN, o + 3 * D_ATTN + LANES)


def _inproj(x2d, mod3, g, w_in_b, seq):
    n = x2d.shape[0]
    tpb = seq // TOKEN_TILE
    wcols = w_in_b.shape[1]
    row = lambda i: (i, 0)
    return pl.pallas_call(
        _inproj_kernel,
        grid=(n // TOKEN_TILE,),
        in_specs=[pl.BlockSpec((TOKEN_TILE, D_MODEL), row),
                  pl.BlockSpec((1, 1, N_MOD * D_MODEL), lambda i: (i // tpb, 0, 0)),
                  pl.BlockSpec((1, D_MODEL), lambda i: (0, 0)),
                  pl.BlockSpec((D_MODEL, wcols), lambda i: (0, 0))],
        out_specs=[pl.BlockSpec((TOKEN_TILE, D_LRU), row),
                   pl.BlockSpec((TOKEN_TILE, D_LRU), row),
                   pl.BlockSpec((TOKEN_TILE, D_ATTN), row),
                   pl.BlockSpec((TOKEN_TILE, D_ATTN), row),
                   pl.BlockSpec((D_ATTN, TOKEN_TILE), lambda i: (0, i)),
                   pl.BlockSpec((TOKEN_TILE, LANES), row)],
        out_shape=[jax.ShapeDtypeStruct((n, D_LRU), F32),
                   jax.ShapeDtypeStruct((n, D_LRU), F32),
                   jax.ShapeDtypeStruct((n, D_ATTN), BF16),
                   jax.ShapeDtypeStruct((n, D_ATTN), BF16),
                   jax.ShapeDtypeStruct((D_ATTN, n), BF16),
                   jax.ShapeDtypeStruct((n, LANES), F32)],
        compiler_params=_params(("arbitrary",)),
        name="in_proj",
    )(x2d, mod3, g, w_in_b)


def _shift_rows(x, d, fill):
    rows = lax.broadcasted_iota(jnp.int32, x.shape, 0)
    return jnp.where(rows >= d, pltpu.roll(x, d, 0), fill)


def _lru_kernel(xl_ref, yl_ref, f_ref, cw_ref, cb_ref, wg_ref, bg_ref, lam_ref, fb_ref, gn_ref,
                o_ref, ccol_ref, prev_ref, hcar_ref, ccar_ref):
    j = pl.program_id(1)

    @pl.when(j == 0)
    def _():
        prev_ref[...] = jnp.zeros_like(prev_ref)
        hcar_ref[...] = jnp.zeros_like(hcar_ref)
        ccar_ref[...] = jnp.zeros_like(ccar_ref)

    x = xl_ref[...]
    t = x.shape[0]
    prev = prev_ref[...]
    rows8 = lax.broadcasted_iota(jnp.int32, prev.shape, 0)
    cw = cw_ref[...]
    xc = x * cw[CONV_WIDTH - 1:CONV_WIDTH, :] + cb_ref[...]
    for d in range(1, CONV_WIDTH):
        rolled = pltpu.roll(x, d, 0)
        head = jnp.where(rows8 < d, pltpu.roll(prev, d, 0), rolled[:SUBLANES])
        shifted = jnp.concatenate([head, rolled[SUBLANES:]], axis=0)
        xc = xc + shifted * cw[CONV_WIDTH - 1 - d:CONV_WIDTH - d, :]
    prev_ref[...] = x[t - SUBLANES:]

    gates = jnp.dot(xc.astype(BF16), wg_ref[...], preferred_element_type=F32) + bg_ref[...]
    r = _sigmoid(gates[:, :D_LRU])
    i = _sigmoid(gates[:, D_LRU:])
    nlam = -lam_ref[...]
    softplus = jnp.maximum(nlam, 0.0) + jnp.log1p(jnp.exp(-jnp.abs(nlam)))
    log_a = (-LRU_C) * r * softplus
    a = jnp.exp(log_a)
    b = jnp.sqrt(1.0 - jnp.exp(2.0 * log_a)) * (i * xc)

    d = 1
    while d < t:
        b = a * _shift_rows(b, d, 0.0) + b
        a = a * _shift_rows(a, d, 1.0)
        d *= 2
    h = b + a * hcar_ref[SUBLANES - 1:SUBLANES, :]
    hcar_ref[...] = h[t - SUBLANES:]

    y = yl_ref[...]
    gelu = 0.5 * y * (1.0 + jnp.tanh(0.7978845608028654 * (y + 0.044715 * (y * y * y))))
    o_ref[...] = _rms(h * gelu, gn_ref[...]).astype(o_ref.dtype)

    z = f_ref[...] + fb_ref[...]
    c = jnp.minimum(z, 0.0) - jnp.log1p(jnp.exp(-jnp.abs(z)))
    d = 1
    while d < t:
        c = c + _shift_rows(c, d, 0.0)
        d *= 2
    c = c + ccar_ref[SUBLANES - 1:SUBLANES, :]
    ccar_ref[...] = c[t - SUBLANES:]
    ccol_ref[...] = c * LOG2_E


def _lru(xl, yl, f, cw, cb, wg, bg, lam, fb, gn, bsz, seq):
    n = xl.shape[0]
    nt = seq // SCAN_TILE
    row = lambda b, j: (b * nt + j, 0)
    const = lambda b, j: (0, 0)
    return pl.pallas_call(
        _lru_kernel,
        grid=(bsz, nt),
        in_specs=[pl.BlockSpec((SCAN_TILE, D_LRU), row),
                  pl.BlockSpec((SCAN_TILE, D_LRU), row),
                  pl.BlockSpec((SCAN_TILE, LANES), row),
                  pl.BlockSpec((CONV_WIDTH, D_LRU), const),
                  pl.BlockSpec((1, D_LRU), const),
                  pl.BlockSpec((D_LRU, 2 * D_LRU), const),
                  pl.BlockSpec((1, 2 * D_LRU), const),
                  pl.BlockSpec((1, D_LRU), const),
                  pl.BlockSpec((1, LANES), const),
                  pl.BlockSpec((1, D_LRU), const)],
        out_specs=[pl.BlockSpec((SCAN_TILE, D_LRU), row),
                   pl.BlockSpec((SCAN_TILE, LANES), row)],
        out_shape=[jax.ShapeDtypeStruct((n, D_LRU), BF16),
                   jax.ShapeDtypeStruct((n, LANES), F32)],
        scratch_shapes=[pltpu.VMEM((SUBLANES, D_LRU), F32),
                        pltpu.VMEM((SUBLANES, D_LRU), F32),
                        pltpu.VMEM((SUBLANES, LANES), F32)],
        compiler_params=_params(("arbitrary", "arbitrary")),
        name="rg_lru",
    )(xl, yl, f, cw, cb, wg, bg, lam, fb, gn)


def _attn_kernel(q_ref, k_ref, vt_ref, ccol_ref, o_ref):
    qi = pl.program_id(1)
    tq = q_ref.shape[0]
    tk = ATTN_KEY_TILE
    n_sub = tq // tk
    lane = lax.broadcasted_iota(jnp.int32, (tq, LANES), 1)
    key_idx = lax.broadcasted_iota(jnp.int32, (tk, tq), 0)
    qry_idx = lax.broadcasted_iota(jnp.int32, (tk, tq), 1)
    nt_dims = (((1,), (1,)), ((), ()))
    n_pairs = D_ATTN // LANES

    q_heads = []
    for pair in range(n_pairs):
        q2 = q_ref[:, pair * LANES:(pair + 1) * LANES]
        for half in range(2):
            in_half = (lane >= half * HEAD_DIM) & (lane < (half + 1) * HEAD_DIM)
            q_heads.append(jnp.where(in_half, q2, jnp.zeros_like(q2)))

    def block(jk, state, masked):
        start = pl.multiple_of(jk * tq, tq)
        units = [(sub, h) for sub in range(n_sub) for h in range(ATTN_HEADS)]

        def scores(unit):
            sub, h = unit
            pair = h // 2
            k2 = k_ref[pl.ds(start + sub * tk, tk), pair * LANES:(pair + 1) * LANES]
            return lax.dot_general(k2, q_heads[h], nt_dims, preferred_element_type=F32)

        s_tiles = {u: scores(units[u]) for u in range(ATTN_LOOKAHEAD)}
        state = list(state)
        for u, (sub, h) in enumerate(units):
            m_old, l_old, acc_old = state[h]
            ck = ccol_ref[pl.ds(start + sub * tk, tk), h:h + 1]
            s = s_tiles.pop(u) - ck
            if masked:
                s = jnp.where(key_idx + sub * tk <= qry_idx, s, NEG_BIG)
            m_new = jnp.maximum(m_old, jnp.max(s, axis=0, keepdims=True))
            p = jnp.exp2(s - m_new)
            alpha = jnp.exp2(m_old - m_new)
            l_new = alpha * l_old + jnp.sum(p, axis=0, keepdims=True)
            if u + ATTN_LOOKAHEAD < len(units):
                s_tiles[u + ATTN_LOOKAHEAD] = scores(units[u + ATTN_LOOKAHEAD])
            vt = vt_ref[h * HEAD_DIM:(h + 1) * HEAD_DIM, pl.ds(start + sub * tk, tk)]
            acc_new = alpha * acc_old + jnp.dot(vt, p.astype(BF16), preferred_element_type=F32)
            state[h] = (m_new, l_new, acc_new)
        return tuple(state)

    init = tuple((jnp.full((1, tq), NEG_BIG, F32), jnp.zeros((1, tq), F32),
                  jnp.zeros((HEAD_DIM, tq), F32)) for _ in range(ATTN_HEADS))
    state = lax.fori_loop(0, qi, functools.partial(block, masked=False), init)
    state = block(qi, state, masked=True)

    for pair in range(n_pairs):
        outs_t = [state[h][2] / state[h][1] for h in (2 * pair, 2 * pair + 1)]
        o_ref[:, pair * LANES:(pair + 1) * LANES] = jnp.concatenate(outs_t, axis=0).T


def _attention(q, k, vt, ccol, bsz, seq):
    n = q.shape[0]
    nq = seq // ATTN_TILE
    return pl.pallas_call(
        _attn_kernel,
        grid=(bsz, nq),
        in_specs=[pl.BlockSpec((ATTN_TILE, D_ATTN), lambda b, i: (b * nq + i, 0)),
                  pl.BlockSpec((seq, D_ATTN), lambda b, i: (b, 0)),
                  pl.BlockSpec((D_ATTN, seq), lambda b, i: (0, b)),
                  pl.BlockSpec((seq, LANES), lambda b, i: (b, 0))],
        out_specs=pl.BlockSpec((ATTN_TILE, D_ATTN), lambda b, i: (b * nq + i, 0)),
        out_shape=jax.ShapeDtypeStruct((n, D_ATTN), F32),
        compiler_params=_params(("arbitrary", "arbitrary")),
        name="fox_attention",
    )(q, k, vt, ccol)


def _outproj_kernel(lru_ref, attn_ref, x_ref, mod_ref, gna_ref, wo_ref, gpost_ref, gpre_ref,
                    wr_ref, br_ref, x1_ref, h2_ref, idx_ref, tw_ref):
    m = mod_ref[0]
    g_m = m[:, 2 * D_MODEL:3 * D_MODEL]
    sh_f = m[:, 3 * D_MODEL:4 * D_MODEL]
    sc_f = m[:, 4 * D_MODEL:5 * D_MODEL]
    attn_n = _rms(attn_ref[...], gna_ref[...]).astype(BF16)
    y = (jnp.dot(lru_ref[...], wo_ref[0:D_LRU, :], preferred_element_type=F32)
         + jnp.dot(attn_n, wo_ref[D_LRU:, :], preferred_element_type=F32))
    x1 = x_ref[...] + g_m * _rms(y, gpost_ref[...])
    x1_ref[...] = x1
    h2 = _rms(x1, gpre_ref[...]) * (1.0 + sc_f) + sh_f
    for cc in range(ROW_CHUNKS):
        h2_ref[pl.ds(cc, h2.shape[0], stride=ROW_CHUNKS), :] = h2[:, cc * LANES:(cc + 1) * LANES]

    logits = jnp.dot(h2, wr_ref[...], preferred_element_type=F32,
                     precision=lax.Precision.HIGHEST) + br_ref[...]
    lane = lax.broadcasted_iota(jnp.int32, logits.shape, 1)
    lane_f = lane.astype(F32)
    cur = logits
    vals, idxs = [], []
    for _ in range(TOP_K):
        mx = jnp.max(cur, axis=-1, keepdims=True)
        ix = jnp.min(jnp.where(cur == mx, lane_f, float(LANES)), axis=-1, keepdims=True)
        vals.append(mx)
        idxs.append(ix)
        cur = jnp.where(lane_f == ix, -jnp.inf, cur)
    exps = [jnp.exp(vv - vals[0]) for vv in vals]
    den = exps[0] + exps[1] + exps[2] + exps[3]
    idx_out = jnp.zeros(logits.shape, F32)
    w_out = jnp.zeros(logits.shape, F32)
    for kk in range(TOP_K):
        idx_out = jnp.where(lane == kk, idxs[kk], idx_out)
        w_out = jnp.where(lane == kk, exps[kk] / den, w_out)
    idx_ref[...] = idx_out.astype(jnp.int32)
    tw_ref[...] = w_out


def _outproj(lru_n, attn, x2d, mod3, gna, wo_b, gpost, gpre, wr, br, seq):
    n = x2d.shape[0]
    tpb = seq // TOKEN_TILE
    row = lambda i: (i, 0)
    const = lambda i: (0, 0)
    return pl.pallas_call(
        _outproj_kernel,
        grid=(n // TOKEN_TILE,),
        in_specs=[pl.BlockSpec((TOKEN_TILE, D_LRU), row),
                  pl.BlockSpec((TOKEN_TILE, D_ATTN), row),
                  pl.BlockSpec((TOKEN_TILE, D_MODEL), row),
                  pl.BlockSpec((1, 1, N_MOD * D_MODEL), lambda i: (i // tpb, 0, 0)),
                  pl.BlockSpec((1, D_ATTN), const),
                  pl.BlockSpec((D_MODEL, D_MODEL), const),
                  pl.BlockSpec((1, D_MODEL), const),
                  pl.BlockSpec((1, D_MODEL), const),
                  pl.BlockSpec((D_MODEL, LANES), const),
                  pl.BlockSpec((1, LANES), const)],
        out_specs=[pl.BlockSpec((TOKEN_TILE, D_MODEL), row),
                   pl.BlockSpec((TOKEN_TILE * ROW_CHUNKS, LANES), row),
                   pl.BlockSpec((TOKEN_TILE, LANES), row),
                   pl.BlockSpec((TOKEN_TILE, LANES), row)],
        out_shape=[jax.ShapeDtypeStruct((n, D_MODEL), F32),
                   jax.ShapeDtypeStruct((n * ROW_CHUNKS, LANES), F32),
                   jax.ShapeDtypeStruct((n, LANES), jnp.int32),
                   jax.ShapeDtypeStruct((n, LANES), F32)],
        compiler_params=_params(("arbitrary",)),
        name="out_proj_router",
    )(lru_n, attn, x2d, mod3, gna, wo_b, gpost, gpre, wr, br)


def _rank_kernel(idx_ref, rank_ref, cnt_ref, base_ref):
    i = pl.program_id(0)

    @pl.when(i == 0)
    def _():
        base_ref[...] = jnp.zeros_like(base_ref)

    idx = idx_ref[...]
    t = idx.shape[0]
    lane = lax.broadcasted_iota(jnp.int32, idx.shape, 1)
    onehots = [lane == idx[:, kk:kk + 1] for kk in range(TOP_K)]
    tot = jnp.zeros(idx.shape, F32)
    for oh in onehots:
        tot = tot + jnp.where(oh, 1.0, 0.0)
    rows = lax.broadcasted_iota(jnp.int32, (t, t), 0)
    cols = lax.broadcasted_iota(jnp.int32, (t, t), 1)
    lower = jnp.where(cols < rows, 1.0, 0.0).astype(BF16)
    before = jnp.dot(lower, tot.astype(BF16), preferred_element_type=F32)
    base = base_ref[0:1, :]
    seen = before + base
    rank = jnp.zeros(idx.shape, F32)
    for kk, oh in enumerate(onehots):
        rk = jnp.sum(jnp.where(oh, seen, 0.0), axis=-1, keepdims=True)
        rank = jnp.where(lane == kk, rk, rank)
    rank_ref[...] = rank.astype(jnp.int32)
    new_base = base + jnp.sum(tot, axis=0, keepdims=True)
    base_ref[...] = jnp.broadcast_to(new_base, base_ref.shape)
    cnt_ref[...] = jnp.broadcast_to(new_base, cnt_ref.shape).astype(jnp.int32)


def _ranks(idx):
    n = idx.shape[0]
    return pl.pallas_call(
        _rank_kernel,
        grid=(n // ROUTE_TILE,),
        in_specs=[pl.BlockSpec((ROUTE_TILE, LANES), lambda i: (i, 0))],
        out_specs=[pl.BlockSpec((ROUTE_TILE, LANES), lambda i: (i, 0)),
                   pl.BlockSpec((SUBLANES, LANES), lambda i: (0, 0))],
        out_shape=[jax.ShapeDtypeStruct((n, LANES), jnp.int32),
                   jax.ShapeDtypeStruct((SUBLANES, LANES), jnp.int32)],
        scratch_shapes=[pltpu.VMEM((SUBLANES, LANES), F32)],
        compiler_params=_params(("arbitrary",)),
        name="route_ranks",
    )(idx)


def _invert_kernel(dest_ref, pad_lo_ref, pad_hi_ref, inv_ref, *, n_pairs):
    def fill_segment(seg, c):
        def fill(p, c2):
            inv_ref[p] = n_pairs + lax.rem(p, ROW_BLOCK)
            return c2
        lax.fori_loop(pad_lo_ref[seg], pad_hi_ref[seg], fill, 0)
        return c
    lax.fori_loop(0, pad_lo_ref.shape[0], fill_segment, 0)

    def place(i, c):
        inv_ref[dest_ref[i]] = i
        return c
    lax.fori_loop(0, n_pairs, place, 0, unroll=INVERT_UNROLL)


def _invert(dest, pad_lo, pad_hi, n_rows):
    n_pairs = dest.shape[0]
    grid_spec = pltpu.PrefetchScalarGridSpec(
        num_scalar_prefetch=3, grid=(1,), in_specs=[],
        out_specs=pl.BlockSpec(memory_space=pltpu.SMEM))
    return pl.pallas_call(
        functools.partial(_invert_kernel, n_pairs=n_pairs),
        grid_spec=grid_spec,
        out_shape=jax.ShapeDtypeStruct((n_rows,), jnp.int32),
        compiler_params=_params(("arbitrary",)),
        name="route_invert",
    )(dest, pad_lo, pad_hi)


def _expert_kernel(bexp_ref, nvalid_ref, nexp_ref, nused_ref, inv_ref,
                   h2_hbm, wup_hbm, bup_ref, wdn_hbm, bdn_ref, out_hbm,
                   xbuf, ybuf, xb_buf, act_buf, wup_f, wdn_f, wup_b, wdn_b, gsem, ssem, wsem,
                   *, n_tokens, n_blocks):
    b = pl.program_id(0)
    nused = nused_ref[0]
    tile_rows = ROW_BLOCK * ROW_CHUNKS

    def weight_copies(e):
        return (pltpu.make_async_copy(wup_hbm.at[e], wup_f, wsem.at[0]),
                pltpu.make_async_copy(wdn_hbm.at[e], wdn_f, wsem.at[1]))

    def tile_of(row):
        start = row * ROW_CHUNKS
        if not isinstance(row, int):
            start = pl.multiple_of(start, ROW_CHUNKS)
        return pl.ds(start, ROW_CHUNKS)

    def gather_copy(blk, slot, r):
        pair = inv_ref[blk * ROW_BLOCK + r]
        tok = lax.rem(pair, n_tokens)
        return pltpu.make_async_copy(h2_hbm.at[tile_of(tok), :], xbuf.at[slot, tile_of(r), :],
                                     gsem.at[slot])

    def scatter_copy(blk, slot, r):
        pair = inv_ref[blk * ROW_BLOCK + r]
        return pltpu.make_async_copy(ybuf.at[slot, tile_of(r), :], out_hbm.at[tile_of(pair), :],
                                     ssem.at[slot])

    def wait_gather(slot):
        pltpu.make_async_copy(h2_hbm.at[pl.ds(0, tile_rows), :], xbuf.at[slot], gsem.at[slot]).wait()

    def wait_scatter(slot, nrows):
        @pl.when(nrows > 0)
        def _():
            nr = pl.multiple_of(nrows * ROW_CHUNKS, ROW_CHUNKS)
            pltpu.make_async_copy(ybuf.at[slot, pl.ds(0, nr), :], out_hbm.at[pl.ds(0, nr), :],
                                  ssem.at[slot]).wait()

    @pl.when(b < nused)
    def _():
        e = bexp_ref[b]
        xslot = lax.rem(b, GATHER_SLOTS)
        slot = lax.rem(b, 2)
        nslot = 1 - slot
        prev_blk = jnp.maximum(b - 1, 0)
        ahead_blk = jnp.minimum(b + GATHER_AHEAD, n_blocks - 1)
        ahead_slot = lax.rem(b + GATHER_AHEAD, GATHER_SLOTS)
        first_of_expert = jnp.logical_or(b == 0, e != bexp_ref[prev_blk])

        @pl.when(b == 0)
        def _():
            for cp in weight_copies(e):
                cp.start()
            for blk in range(GATHER_AHEAD):
                def start_row(r, c, blk=blk):
                    gather_copy(min(blk, n_blocks - 1), blk, r).start()
                    return c
                lax.fori_loop(0, ROW_BLOCK, start_row, 0)

        @pl.when(first_of_expert)
        def _():
            for cp in weight_copies(e):
                cp.wait()
            wup_b[...] = wup_f[...].astype(BF16)
            wdn_b[...] = wdn_f[...].astype(BF16)
            nxt = nexp_ref[b]

            @pl.when(nxt >= 0)
            def _():
                for cp in weight_copies(nxt):
                    cp.start()

        wait_gather(xslot)

        n_prev = jnp.where(b >= 1, nvalid_ref[prev_blk], 0)

        def start_row_copies(lo, hi):
            for r in range(lo, hi):
                gather_copy(ahead_blk, ahead_slot, r).start()

                @pl.when(r < n_prev)
                def _():
                    scatter_copy(prev_blk, nslot, r).start(priority=SCATTER_DMA_PRIORITY)

        n_tiles_e = D_EXPERT // MXU_TILE
        n_tiles_m = D_MODEL // MXU_TILE
        n_stages = 2 * n_tiles_e + n_tiles_m
        per_stage = -(-ROW_BLOCK // n_stages)
        stage = [0]

        def next_stage_copies():
            lo = min(stage[0] * per_stage, ROW_BLOCK)
            hi = min(lo + per_stage, ROW_BLOCK)
            stage[0] += 1
            start_row_copies(lo, hi)

        for cc in range(ROW_CHUNKS):
            xb_buf[:, cc * LANES:(cc + 1) * LANES] = (
                xbuf[xslot, pl.ds(cc, ROW_BLOCK, stride=ROW_CHUNKS), :].astype(BF16))

        for j in range(n_tiles_e):
            lo, hi = j * MXU_TILE, (j + 1) * MXU_TILE
            next_stage_copies()
            glu = jnp.dot(xb_buf[...], wup_b[:, lo:hi], preferred_element_type=F32) + bup_ref[0, :, lo:hi]
            next_stage_copies()
            lin = (jnp.dot(xb_buf[...], wup_b[:, D_EXPERT + lo:D_EXPERT + hi], preferred_element_type=F32)
                   + bup_ref[0, :, D_EXPERT + lo:D_EXPERT + hi])
            glu = jnp.minimum(glu, SWIGLU_LIMIT)
            lin = jnp.clip(lin, -SWIGLU_LIMIT, SWIGLU_LIMIT)
            act = glu * _sigmoid(SWIGLU_ALPHA * glu) * (lin + 1.0)
            act_buf[:, lo:hi] = act.astype(BF16)

        @pl.when(b >= 2)
        def _():
            wait_scatter(slot, nvalid_ref[jnp.maximum(b - 2, 0)])

        for j in range(n_tiles_m):
            lo, hi = j * MXU_TILE, (j + 1) * MXU_TILE
            next_stage_copies()
            y = jnp.dot(act_buf[...], wdn_b[:, lo:hi], preferred_element_type=F32) + bdn_ref[0, :, lo:hi]
            for cc in range(MXU_TILE // LANES):
                ybuf[slot, pl.ds(lo // LANES + cc, ROW_BLOCK, stride=ROW_CHUNKS), :] = (
                    y[:, cc * LANES:(cc + 1) * LANES])

        @pl.when(b == nused - 1)
        def _():
            n_cur = nvalid_ref[b]

            def start_row(r, c):
                scatter_copy(b, slot, r).start(priority=SCATTER_DMA_PRIORITY)
                return c
            lax.fori_loop(0, n_cur, start_row, 0)
            wait_scatter(nslot, n_prev)
            wait_scatter(slot, n_cur)
            for ahead in range(1, GATHER_AHEAD + 1):
                wait_gather(lax.rem(b + ahead, GATHER_SLOTS))


def _experts(bexp, nvalid, nexp, nused, inv, h2t, w_up, b_up, w_down, b_down):
    n = h2t.shape[0] // ROW_CHUNKS
    n_blocks = bexp.shape[0]
    tile_rows = ROW_BLOCK * ROW_CHUNKS
    bias_map = lambda b, be, nv, nx, nu, iv: (be[b], 0, 0)
    grid_spec = pltpu.PrefetchScalarGridSpec(
        num_scalar_prefetch=5,
        grid=(n_blocks,),
        in_specs=[pl.BlockSpec(memory_space=pl.ANY),
                  pl.BlockSpec(memory_space=pl.ANY),
                  pl.BlockSpec((1, 1, 2 * D_EXPERT), bias_map),
                  pl.BlockSpec(memory_space=pl.ANY),
                  pl.BlockSpec((1, 1, D_MODEL), bias_map)],
        out_specs=pl.BlockSpec(memory_space=pl.ANY),
        scratch_shapes=[pltpu.VMEM((GATHER_SLOTS, tile_rows, LANES), F32),
                        pltpu.VMEM((2, tile_rows, LANES), F32),
                        pltpu.VMEM((ROW_BLOCK, D_MODEL), BF16),
                        pltpu.VMEM((ROW_BLOCK, D_EXPERT), BF16),
                        pltpu.VMEM((D_MODEL, 2 * D_EXPERT), F32),
                        pltpu.VMEM((D_EXPERT, D_MODEL), F32),
                        pltpu.VMEM((D_MODEL, 2 * D_EXPERT), BF16),
                        pltpu.VMEM((D_EXPERT, D_MODEL), BF16),
                        pltpu.SemaphoreType.DMA((GATHER_SLOTS,)),
                        pltpu.SemaphoreType.DMA((2,)),
                        pltpu.SemaphoreType.DMA((2,))],
    )
    return pl.pallas_call(
        functools.partial(_expert_kernel, n_tokens=n, n_blocks=n_blocks),
        grid_spec=grid_spec,
        out_shape=jax.ShapeDtypeStruct((n * TOP_K * ROW_CHUNKS, LANES), F32),
        compiler_params=_params(("arbitrary",)),
        name="moe_experts",
    )(bexp, nvalid, nexp, nused, inv, h2t, w_up, b_up.reshape(N_EXPERTS, 1, 2 * D_EXPERT),
      w_down, b_down.reshape(N_EXPERTS, 1, D_MODEL))


def _combine_kernel(r0_ref, r1_ref, r2_ref, r3_ref, tw_ref, x1_ref, mod_ref, g_ref, o_ref):
    tw = tw_ref[...]
    t = tw.shape[0]
    r_refs = (r0_ref, r1_ref, r2_ref, r3_ref)
    chunks = []
    sumsq = jnp.zeros((t, 1), F32)
    for cc in range(ROW_CHUNKS):
        rows_cc = pl.ds(cc, t, stride=ROW_CHUNKS)
        y = r_refs[0][rows_cc, :] * tw[:, 0:1]
        for kk in range(1, TOP_K):
            y = y + r_refs[kk][rows_cc, :] * tw[:, kk:kk + 1]
        chunks.append(y)
        sumsq = sumsq + jnp.sum(y * y, axis=-1, keepdims=True)
    inv_rms = lax.rsqrt(sumsq * (1.0 / D_MODEL) + RMS_EPS)
    m = mod_ref[0]
    for cc, y in enumerate(chunks):
        lo, hi = cc * LANES, (cc + 1) * LANES
        g_f = m[:, 5 * D_MODEL + lo:5 * D_MODEL + hi]
        o_ref[:, lo:hi] = x1_ref[:, lo:hi] + g_f * (y * inv_rms * g_ref[:, lo:hi])


def _combine(rows, tw, x1, mod3, g, seq):
    n = x1.shape[0]
    tpb = seq // COMBINE_TILE
    nt = n // COMBINE_TILE
    row = lambda i: (i, 0)
    slot_specs = [pl.BlockSpec((COMBINE_TILE * ROW_CHUNKS, LANES),
                               functools.partial(lambda i, kk: (kk * nt + i, 0), kk=kk))
                  for kk in range(TOP_K)]
    return pl.pallas_call(
        _combine_kernel,
        grid=(nt,),
        in_specs=slot_specs + [
            pl.BlockSpec((COMBINE_TILE, LANES), row),
            pl.BlockSpec((COMBINE_TILE, D_MODEL), row),
            pl.BlockSpec((1, 1, N_MOD * D_MODEL), lambda i: (i // tpb, 0, 0)),
            pl.BlockSpec((1, D_MODEL), lambda i: (0, 0))],
        out_specs=pl.BlockSpec((COMBINE_TILE, D_MODEL), row),
        out_shape=jax.ShapeDtypeStruct((n, D_MODEL), F32),
        compiler_params=_params(("arbitrary",)),
        name="moe_combine",
    )(rows, rows, rows, rows, tw, x1, mod3, g)


def _block_diag(w):
    g, d, _ = w.shape
    eye = jnp.eye(g, dtype=w.dtype)
    return jnp.einsum("gij,gh->gihj", w, eye).reshape(g * d, g * d)


def _layer(x2d, mod, bsz, seq, norm_mix_pre, norm_mix_post, w_in, conv_w, conv_b, lru_wa, lru_ba,
           lru_wx, lru_bx, lru_lambda, attn_fb, gn_lru, gn_attn, w_out, norm_ffn_pre, norm_ffn_post,
           w_router, b_router, w_up, b_up, w_down, b_down):
    n = x2d.shape[0]
    mod3 = mod.reshape(bsz, 1, N_MOD * D_MODEL)
    d_in = w_in.shape[1]
    w_in_b = jnp.pad(w_in, ((0, 0), (0, 2 * D_LRU + 3 * D_ATTN + LANES - d_in))).astype(BF16)
    xl, yl, q, k, vt, f = _inproj(x2d, mod3, norm_mix_pre.reshape(1, -1), w_in_b, seq)

    wg = jnp.concatenate([_block_diag(lru_wa), _block_diag(lru_wx)], axis=1).astype(BF16)
    bg = jnp.concatenate([lru_ba.reshape(1, -1), lru_bx.reshape(1, -1)], axis=1)
    fb = jnp.pad(attn_fb.reshape(1, -1), ((0, 0), (0, LANES - ATTN_HEADS)))
    lru_n, ccol = _lru(xl, yl, f, conv_w, conv_b.reshape(1, -1), wg, bg,
                       lru_lambda.reshape(1, -1), fb, gn_lru.reshape(1, -1), bsz, seq)

    attn = _attention(q, k, vt, ccol, bsz, seq)

    wr = jnp.pad(w_router, ((0, 0), (0, LANES - N_EXPERTS)))
    br = jnp.pad(b_router.reshape(1, -1), ((0, 0), (0, LANES - N_EXPERTS)), constant_values=NEG_BIG)
    x1, h2, idx, tw = _outproj(lru_n, attn, x2d, mod3, gn_attn.reshape(1, -1), w_out.astype(BF16),
                               norm_mix_post.reshape(1, -1), norm_ffn_pre.reshape(1, -1), wr, br, seq)

    rank, cnt = _ranks(idx)
    counts = cnt[0, :N_EXPERTS]
    padded = (counts + ROW_BLOCK - 1) // ROW_BLOCK * ROW_BLOCK
    pends = jnp.cumsum(padded)
    pstarts = pends - padded
    n_blocks = n * TOP_K // ROW_BLOCK + N_EXPERTS
    n_rows = n_blocks * ROW_BLOCK
    blk_start = jnp.arange(n_blocks, dtype=jnp.int32) * ROW_BLOCK
    eids = jnp.arange(N_EXPERTS, dtype=jnp.int32)
    bexp = jnp.minimum(jnp.sum((pends[None, :] <= blk_start[:, None]).astype(jnp.int32), axis=1),
                       N_EXPERTS - 1)
    onehot_b = bexp[:, None] == eids[None, :]
    blk_count = jnp.sum(jnp.where(onehot_b, counts[None, :], 0), axis=1)
    blk_pstart = jnp.sum(jnp.where(onehot_b, pstarts[None, :], 0), axis=1)
    nused = (pends[-1] // ROW_BLOCK).astype(jnp.int32)
    nvalid = jnp.clip(blk_count - (blk_start - blk_pstart), 0, ROW_BLOCK)
    nvalid = jnp.where(jnp.arange(n_blocks) < nused, nvalid, 0).astype(jnp.int32)
    later_used = (counts[None, :] > 0) & (eids[None, :] > eids[:, None])
    next_used = jnp.min(jnp.where(later_used, eids[None, :], N_EXPERTS), axis=1)
    next_used = jnp.where(next_used == N_EXPERTS, -1, next_used)
    nexp = jnp.sum(jnp.where(onehot_b, next_used[None, :], 0), axis=1).astype(jnp.int32)

    e_flat = idx[:, :TOP_K].T.reshape(-1)
    pair_pstart = jnp.sum(jnp.where(e_flat[:, None] == eids[None, :], pstarts[None, :], 0), axis=1)
    dest = pair_pstart + rank[:, :TOP_K].T.reshape(-1)
    pad_lo = jnp.concatenate([pstarts + counts, pends[-1:]]).astype(jnp.int32)
    pad_hi = jnp.concatenate([pends, jnp.full((1,), n_rows, jnp.int32)]).astype(jnp.int32)
    inv = _invert(dest.astype(jnp.int32), pad_lo, pad_hi, n_rows)

    rows = _experts(bexp, nvalid, nexp, nused.reshape(1), inv, h2, w_up, b_up, w_down, b_down)
    return _combine(rows, tw, x1, mod3, norm_ffn_post.reshape(1, -1), seq)


def kernel(x, c, w_ada, b_ada, norm_mix_pre, norm_mix_post, w_in, conv_w, conv_b, lru_wa, lru_ba,
           lru_wx, lru_bx, lru_lambda, attn_fb, gn_lru, gn_attn, w_out, norm_ffn_pre, norm_ffn_post,
           w_router, b_router, w_up, b_up, w_down, b_down):
    bsz, seq, d = x.shape
    x2d = x.reshape(bsz * seq, d)
    for l in range(w_ada.shape[0]):
        mod = _ada(c, w_ada[l], b_ada[l])
        x2d = _layer(x2d, mod, bsz, seq, norm_mix_pre[l], norm_mix_post[l], w_in[l], conv_w[l],
                     conv_b[l], lru_wa[l], lru_ba[l], lru_wx[l], lru_bx[l], lru_lambda[l],
                     attn_fb[l], gn_lru[l], gn_attn[l], w_out[l], norm_ffn_pre[l], norm_ffn_post[l],
                     w_router[l], b_router[l], w_up[l], b_up[l], w_down[l], b_down[l])
    return x2d.reshape(bsz, seq, d)
```

```python
import functools

import jax
import jax.numpy as jnp
from jax import lax
from jax.experimental import pallas as pl
from jax.experimental.pallas import tpu as pltpu

F32 = jnp.float32
BF16 = jnp.bfloat16

D_MODEL = 1024
D_LRU = 512
LRU_BLOCKS = 8
CONV_WIDTH = 4
LRU_C = 8.0
D_ATTN = 512
ATTN_HEADS = 8
HEAD_DIM = 64
N_EXPERTS = 32
TOP_K = 4
D_EXPERT = 1024
SWIGLU_ALPHA = 1.702
SWIGLU_LIMIT = 7.0
N_MOD = 6
RMS_EPS = 1e-6

LANES = 128
SUBLANES = 8
MXU_TILE = 256
VMEM_LIMIT = 48 * 1024 * 1024

TOKEN_TILE = 512
SCAN_TILE = 256
ATTN_TILE = 256
ATTN_KEY_TILE = 256
ATTN_LOOKAHEAD = 4
ROUTE_TILE = 512
ROW_BLOCK = 256
GATHER_AHEAD = 2
BLOCKS_PER_STEP = 4
GATHER_SLOTS = 4
assert GATHER_SLOTS > GATHER_AHEAD and BLOCKS_PER_STEP % GATHER_SLOTS == 0 and BLOCKS_PER_STEP % 2 == 0
INVERT_UNROLL = 8
GATHER_DMA_QUEUES = (0,)
SCATTER_DMA_QUEUES = (1,)
COMBINE_TILE = 256
NEG_BIG = -1e30
LOG2_E = 1.4426950408889634
ROW_CHUNKS = D_MODEL // LANES
assert ROW_CHUNKS == SUBLANES


def _sigmoid(z):
    return 1.0 / (1.0 + jnp.exp(-z))


def _rms(x, g):
    ms = jnp.mean(x * x, axis=-1, keepdims=True)
    return x * lax.rsqrt(ms + RMS_EPS) * g


def _params(sem):
    return pltpu.CompilerParams(dimension_semantics=sem, vmem_limit_bytes=VMEM_LIMIT)


def _ada_kernel(c_ref, w_ref, b_ref, o_ref):
    c = c_ref[...]
    cond = c * _sigmoid(c)
    o_ref[...] = jnp.dot(cond, w_ref[...], preferred_element_type=F32,
                         precision=lax.Precision.HIGHEST) + b_ref[...]


def _ada(c, w, b):
    bsz = c.shape[0]
    n_out = w.shape[1]
    return pl.pallas_call(
        _ada_kernel,
        grid=(n_out // D_MODEL,),
        in_specs=[pl.BlockSpec((bsz, D_MODEL), lambda j: (0, 0)),
                  pl.BlockSpec((D_MODEL, D_MODEL), lambda j: (0, j)),
                  pl.BlockSpec((1, D_MODEL), lambda j: (0, j))],
        out_specs=pl.BlockSpec((bsz, D_MODEL), lambda j: (0, j)),
        out_shape=jax.ShapeDtypeStruct((bsz, n_out), F32),
        compiler_params=_params(("arbitrary",)),
        name="ada_mod",
    )(c, w, b.reshape(1, n_out))


def _inproj_kernel(x_ref, mod_ref, g_ref, w_ref, xl_ref, yl_ref, q_ref, k_ref, vt_ref, f_ref):
    x = x_ref[...]
    m = mod_ref[0]
    sh = m[:, 0:D_MODEL]
    sc = m[:, D_MODEL:2 * D_MODEL]
    h = _rms(x, g_ref[...]) * (1.0 + sc) + sh
    hb = h.astype(BF16)

    def proj(lo, hi):
        return jnp.dot(hb, w_ref[:, lo:hi], preferred_element_type=F32)

    xl_ref[...] = proj(0, D_LRU)
    yl_ref[...] = proj(D_LRU, 2 * D_LRU)
    o = 2 * D_LRU
    q_ref[...] = (proj(o, o + D_ATTN) * (LOG2_E * HEAD_DIM ** -0.5)).astype(BF16)
    k_ref[...] = proj(o + D_ATTN, o + 2 * D_ATTN).astype(BF16)
    vt_ref[...] = proj(o + 2 * D_ATTN, o + 3 * D_ATTN).T.astype(BF16)
    f_ref[...] = proj(o + 3 * D_ATTN, o + 3 * D_ATTN + LANES)


def _inproj(x2d, mod3, g, w_in_b, seq):
    n = x2d.shape[0]
    tpb = seq // TOKEN_TILE
    wcols = w_in_b.shape[1]
    row = lambda i: (i, 0)
    return pl.pallas_call(
        _inproj_kernel,
        grid=(n // TOKEN_TILE,),
        in_specs=[pl.BlockSpec((TOKEN_TILE, D_MODEL), row),
                  pl.BlockSpec((1, 1, N_MOD * D_MODEL), lambda i: (i // tpb, 0, 0)),
                  pl.BlockSpec((1, D_MODEL), lambda i: (0, 0)),
                  pl.BlockSpec((D_MODEL, wcols), lambda i: (0, 0))],
        out_specs=[pl.BlockSpec((TOKEN_TILE, D_LRU), row),
                   pl.BlockSpec((TOKEN_TILE, D_LRU), row),
                   pl.BlockSpec((TOKEN_TILE, D_ATTN), row),
                   pl.BlockSpec((TOKEN_TILE, D_ATTN), row),
                   pl.BlockSpec((D_ATTN, TOKEN_TILE), lambda i: (0, i)),
                   pl.BlockSpec((TOKEN_TILE, LANES), row)],
        out_shape=[jax.ShapeDtypeStruct((n, D_LRU), F32),
                   jax.ShapeDtypeStruct((n, D_LRU), F32),
                   jax.ShapeDtypeStruct((n, D_ATTN), BF16),
                   jax.ShapeDtypeStruct((n, D_ATTN), BF16),
                   jax.ShapeDtypeStruct((D_ATTN, n), BF16),
                   jax.ShapeDtypeStruct((n, LANES), F32)],
        compiler_params=_params(("arbitrary",)),
        name="in_proj",
    )(x2d, mod3, g, w_in_b)


def _shift_rows(x, d, fill):
    rows = lax.broadcasted_iota(jnp.int32, x.shape, 0)
    return jnp.where(rows >= d, pltpu.roll(x, d, 0), fill)


def _lru_kernel(xl_ref, yl_ref, f_ref, cw_ref, cb_ref, wg_ref, bg_ref, lam_ref, fb_ref, gn_ref,
                o_ref, ccol_ref, prev_ref, hcar_ref, ccar_ref):
    j = pl.program_id(1)

    @pl.when(j == 0)
    def _():
        prev_ref[...] = jnp.zeros_like(prev_ref)
        hcar_ref[...] = jnp.zeros_like(hcar_ref)
        ccar_ref[...] = jnp.zeros_like(ccar_ref)

    x = xl_ref[...]
    t = x.shape[0]
    prev = prev_ref[...]
    rows8 = lax.broadcasted_iota(jnp.int32, prev.shape, 0)
    cw = cw_ref[...]
    xc = x * cw[CONV_WIDTH - 1:CONV_WIDTH, :] + cb_ref[...]
    for d in range(1, CONV_WIDTH):
        rolled = pltpu.roll(x, d, 0)
        head = jnp.where(rows8 < d, pltpu.roll(prev, d, 0), rolled[:SUBLANES])
        shifted = jnp.concatenate([head, rolled[SUBLANES:]], axis=0)
        xc = xc + shifted * cw[CONV_WIDTH - 1 - d:CONV_WIDTH - d, :]
    prev_ref[...] = x[t - SUBLANES:]

    gates = jnp.dot(xc.astype(BF16), wg_ref[...], preferred_element_type=F32) + bg_ref[...]
    r = _sigmoid(gates[:, :D_LRU])
    i = _sigmoid(gates[:, D_LRU:])
    nlam = -lam_ref[...]
    softplus = jnp.maximum(nlam, 0.0) + jnp.log1p(jnp.exp(-jnp.abs(nlam)))
    log_a = (-LRU_C) * r * softplus
    a = jnp.exp(log_a)
    b = jnp.sqrt(1.0 - jnp.exp(2.0 * log_a)) * (i * xc)

    d = 1
    while d < t:
        b = a * _shift_rows(b, d, 0.0) + b
        a = a * _shift_rows(a, d, 1.0)
        d *= 2
    h = b + a * hcar_ref[SUBLANES - 1:SUBLANES, :]
    hcar_ref[...] = h[t - SUBLANES:]

    y = yl_ref[...]
    gelu = 0.5 * y * (1.0 + jnp.tanh(0.7978845608028654 * (y + 0.044715 * (y * y * y))))
    o_ref[...] = _rms(h * gelu, gn_ref[...]).astype(o_ref.dtype)

    z = f_ref[...] + fb_ref[...]
    c = jnp.minimum(z, 0.0) - jnp.log1p(jnp.exp(-jnp.abs(z)))
    d = 1
    while d < t:
        c = c + _shift_rows(c, d, 0.0)
        d *= 2
    c = c + ccar_ref[SUBLANES - 1:SUBLANES, :]
    ccar_ref[...] = c[t - SUBLANES:]
    ccol_ref[...] = c * LOG2_E


def _lru(xl, yl, f, cw, cb, wg, bg, lam, fb, gn, bsz, seq):
    n = xl.shape[0]
    nt = seq // SCAN_TILE
    row = lambda b, j: (b * nt + j, 0)
    const = lambda b, j: (0, 0)
    return pl.pallas_call(
        _lru_kernel,
        grid=(bsz, nt),
        in_specs=[pl.BlockSpec((SCAN_TILE, D_LRU), row),
                  pl.BlockSpec((SCAN_TILE, D_LRU), row),
                  pl.BlockSpec((SCAN_TILE, LANES), row),
                  pl.BlockSpec((CONV_WIDTH, D_LRU), const),
                  pl.BlockSpec((1, D_LRU), const),
                  pl.BlockSpec((D_LRU, 2 * D_LRU), const),
                  pl.BlockSpec((1, 2 * D_LRU), const),
                  pl.BlockSpec((1, D_LRU), const),
                  pl.BlockSpec((1, LANES), const),
                  pl.BlockSpec((1, D_LRU), const)],
        out_specs=[pl.BlockSpec((SCAN_TILE, D_LRU), row),
                   pl.BlockSpec((SCAN_TILE, LANES), row)],
        out_shape=[jax.ShapeDtypeStruct((n, D_LRU), BF16),
                   jax.ShapeDtypeStruct((n, LANES), F32)],
        scratch_shapes=[pltpu.VMEM((SUBLANES, D_LRU), F32),
                        pltpu.VMEM((SUBLANES, D_LRU), F32),
                        pltpu.VMEM((SUBLANES, LANES), F32)],
        compiler_params=_params(("arbitrary", "arbitrary")),
        name="rg_lru",
    )(xl, yl, f, cw, cb, wg, bg, lam, fb, gn)


def _attn_kernel(q_ref, k_ref, vt_ref, ccol_ref, o_ref):
    qi = pl.program_id(1)
    tq = q_ref.shape[0]
    tk = ATTN_KEY_TILE
    n_sub = tq // tk
    lane = lax.broadcasted_iota(jnp.int32, (tq, LANES), 1)
    key_idx = lax.broadcasted_iota(jnp.int32, (tk, tq), 0)
    qry_idx = lax.broadcasted_iota(jnp.int32, (tk, tq), 1)
    nt_dims = (((1,), (1,)), ((), ()))
    n_pairs = D_ATTN // LANES

    q_heads = []
    for pair in range(n_pairs):
        q2 = q_ref[:, pair * LANES:(pair + 1) * LANES]
        for half in range(2):
            in_half = (lane >= half * HEAD_DIM) & (lane < (half + 1) * HEAD_DIM)
            q_heads.append(jnp.where(in_half, q2, jnp.zeros_like(q2)))

    def block(jk, state, masked):
        start = pl.multiple_of(jk * tq, tq)
        units = [(sub, h) for sub in range(n_sub) for h in range(ATTN_HEADS)]

        def scores(unit):
            sub, h = unit
            pair = h // 2
            k2 = k_ref[pl.ds(start + sub * tk, tk), pair * LANES:(pair + 1) * LANES]
            return lax.dot_general(k2, q_heads[h], nt_dims, preferred_element_type=F32)

        s_tiles = {u: scores(units[u]) for u in range(ATTN_LOOKAHEAD)}
        state = list(state)
        for u, (sub, h) in enumerate(units):
            m_old, l_old, acc_old = state[h]
            ck = ccol_ref[pl.ds(start + sub * tk, tk), h:h + 1]
            s = s_tiles.pop(u) - ck
            if masked:
                s = jnp.where(key_idx + sub * tk <= qry_idx, s, NEG_BIG)
            m_new = jnp.maximum(m_old, jnp.max(s, axis=0, keepdims=True))
            p = jnp.exp2(s - m_new)
            alpha = jnp.exp2(m_old - m_new)
            l_new = alpha * l_old + jnp.sum(p, axis=0, keepdims=True)
            if u + ATTN_LOOKAHEAD < len(units):
                s_tiles[u + ATTN_LOOKAHEAD] = scores(units[u + ATTN_LOOKAHEAD])
            vt = vt_ref[h * HEAD_DIM:(h + 1) * HEAD_DIM, pl.ds(start + sub * tk, tk)]
            acc_new = alpha * acc_old + jnp.dot(vt, p.astype(BF16), preferred_element_type=F32)
            state[h] = (m_new, l_new, acc_new)
        return tuple(state)

    init = tuple((jnp.full((1, tq), NEG_BIG, F32), jnp.zeros((1, tq), F32),
                  jnp.zeros((HEAD_DIM, tq), F32)) for _ in range(ATTN_HEADS))
    state = lax.fori_loop(0, qi, functools.partial(block, masked=False), init)
    state = block(qi, state, masked=True)

    for pair in range(n_pairs):
        outs_t = [state[h][2] / state[h][1] for h in (2 * pair, 2 * pair + 1)]
        o_ref[:, pair * LANES:(pair + 1) * LANES] = jnp.concatenate(outs_t, axis=0).T


def _attention(q, k, vt, ccol, bsz, seq):
    n = q.shape[0]
    nq = seq // ATTN_TILE
    return pl.pallas_call(
        _attn_kernel,
        grid=(bsz, nq),
        in_specs=[pl.BlockSpec((ATTN_TILE, D_ATTN), lambda b, i: (b * nq + i, 0)),
                  pl.BlockSpec((seq, D_ATTN), lambda b, i: (b, 0)),
                  pl.BlockSpec((D_ATTN, seq), lambda b, i: (0, b)),
                  pl.BlockSpec((seq, LANES), lambda b, i: (b, 0))],
        out_specs=pl.BlockSpec((ATTN_TILE, D_ATTN), lambda b, i: (b * nq + i, 0)),
        out_shape=jax.ShapeDtypeStruct((n, D_ATTN), F32),
        compiler_params=_params(("arbitrary", "arbitrary")),
        name="fox_attention",
    )(q, k, vt, ccol)


def _outproj_kernel(lru_ref, attn_ref, x_ref, mod_ref, gna_ref, wo_ref, gpost_ref, gpre_ref,
                    wr_ref, br_ref, x1_ref, h2_ref, idx_ref, tw_ref):
    m = mod_ref[0]
    g_m = m[:, 2 * D_MODEL:3 * D_MODEL]
    sh_f = m[:, 3 * D_MODEL:4 * D_MODEL]
    sc_f = m[:, 4 * D_MODEL:5 * D_MODEL]
    attn_n = _rms(attn_ref[...], gna_ref[...]).astype(BF16)
    y = (jnp.dot(lru_ref[...], wo_ref[0:D_LRU, :], preferred_element_type=F32)
         + jnp.dot(attn_n, wo_ref[D_LRU:, :], preferred_element_type=F32))
    x1 = x_ref[...] + g_m * _rms(y, gpost_ref[...])
    x1_ref[...] = x1
    h2 = _rms(x1, gpre_ref[...]) * (1.0 + sc_f) + sh_f
    for cc in range(ROW_CHUNKS):
        h2_ref[pl.ds(cc, h2.shape[0], stride=ROW_CHUNKS), :] = h2[:, cc * LANES:(cc + 1) * LANES]

    h2_hi = h2.astype(BF16)
    h2_lo = (h2 - h2_hi.astype(F32)).astype(BF16)
    hi_part = jnp.dot(h2_hi, wr_ref[...], preferred_element_type=F32)
    lo_part = jnp.dot(h2_lo, wr_ref[:, 0:LANES], preferred_element_type=F32)
    logits = hi_part[:, 0:LANES] + hi_part[:, LANES:] + lo_part + br_ref[...]
    lane = lax.broadcasted_iota(jnp.int32, logits.shape, 1)
    lane_f = lane.astype(F32)
    cur = logits
    vals, idxs = [], []
    for _ in range(TOP_K):
        mx = jnp.max(cur, axis=-1, keepdims=True)
        ix = jnp.min(jnp.where(cur == mx, lane_f, float(LANES)), axis=-1, keepdims=True)
        vals.append(mx)
        idxs.append(ix)
        cur = jnp.where(lane_f == ix, -jnp.inf, cur)
    exps = [jnp.exp(vv - vals[0]) for vv in vals]
    den = exps[0] + exps[1] + exps[2] + exps[3]
    idx_out = jnp.zeros(logits.shape, F32)
    w_out = jnp.zeros(logits.shape, F32)
    for kk in range(TOP_K):
        idx_out = jnp.where(lane == kk, idxs[kk], idx_out)
        w_out = jnp.where(lane == kk, exps[kk] / den, w_out)
    idx_ref[...] = idx_out.astype(jnp.int32)
    tw_ref[...] = w_out


def _outproj(lru_n, attn, x2d, mod3, gna, wo_b, gpost, gpre, wr, br, seq):
    n = x2d.shape[0]
    tpb = seq // TOKEN_TILE
    row = lambda i: (i, 0)
    const = lambda i: (0, 0)
    return pl.pallas_call(
        _outproj_kernel,
        grid=(n // TOKEN_TILE,),
        in_specs=[pl.BlockSpec((TOKEN_TILE, D_LRU), row),
                  pl.BlockSpec((TOKEN_TILE, D_ATTN), row),
                  pl.BlockSpec((TOKEN_TILE, D_MODEL), row),
                  pl.BlockSpec((1, 1, N_MOD * D_MODEL), lambda i: (i // tpb, 0, 0)),
                  pl.BlockSpec((1, D_ATTN), const),
                  pl.BlockSpec((D_MODEL, D_MODEL), const),
                  pl.BlockSpec((1, D_MODEL), const),
                  pl.BlockSpec((1, D_MODEL), const),
                  pl.BlockSpec((D_MODEL, 2 * LANES), const),
                  pl.BlockSpec((1, LANES), const)],
        out_specs=[pl.BlockSpec((TOKEN_TILE, D_MODEL), row),
                   pl.BlockSpec((TOKEN_TILE * ROW_CHUNKS, LANES), row),
                   pl.BlockSpec((TOKEN_TILE, LANES), row),
                   pl.BlockSpec((TOKEN_TILE, LANES), row)],
        out_shape=[jax.ShapeDtypeStruct((n, D_MODEL), F32),
                   jax.ShapeDtypeStruct((n * ROW_CHUNKS, LANES), F32),
                   jax.ShapeDtypeStruct((n, LANES), jnp.int32),
                   jax.ShapeDtypeStruct((n, LANES), F32)],
        compiler_params=_params(("arbitrary",)),
        name="out_proj_router",
    )(lru_n, attn, x2d, mod3, gna, wo_b, gpost, gpre, wr, br)


def _rank_kernel(idx_ref, rank_ref, cnt_ref, base_ref):
    i = pl.program_id(0)

    @pl.when(i == 0)
    def _():
        base_ref[...] = jnp.zeros_like(base_ref)

    idx = idx_ref[...]
    t = idx.shape[0]
    lane = lax.broadcasted_iota(jnp.int32, idx.shape, 1)
    onehots = [lane == idx[:, kk:kk + 1] for kk in range(TOP_K)]
    tot = jnp.zeros(idx.shape, F32)
    for oh in onehots:
        tot = tot + jnp.where(oh, 1.0, 0.0)
    rows = lax.broadcasted_iota(jnp.int32, (t, t), 0)
    cols = lax.broadcasted_iota(jnp.int32, (t, t), 1)
    lower = jnp.where(cols < rows, 1.0, 0.0).astype(BF16)
    before = jnp.dot(lower, tot.astype(BF16), preferred_element_type=F32)
    base = base_ref[0:1, :]
    seen = before + base
    rank = jnp.zeros(idx.shape, F32)
    for kk, oh in enumerate(onehots):
        rk = jnp.sum(jnp.where(oh, seen, 0.0), axis=-1, keepdims=True)
        rank = jnp.where(lane == kk, rk, rank)
    rank_ref[...] = rank.astype(jnp.int32)
    new_base = base + jnp.sum(tot, axis=0, keepdims=True)
    base_ref[...] = jnp.broadcast_to(new_base, base_ref.shape)
    cnt_ref[...] = jnp.broadcast_to(new_base, cnt_ref.shape).astype(jnp.int32)


def _ranks(idx):
    n = idx.shape[0]
    return pl.pallas_call(
        _rank_kernel,
        grid=(n // ROUTE_TILE,),
        in_specs=[pl.BlockSpec((ROUTE_TILE, LANES), lambda i: (i, 0))],
        out_specs=[pl.BlockSpec((ROUTE_TILE, LANES), lambda i: (i, 0)),
                   pl.BlockSpec((SUBLANES, LANES), lambda i: (0, 0))],
        out_shape=[jax.ShapeDtypeStruct((n, LANES), jnp.int32),
                   jax.ShapeDtypeStruct((SUBLANES, LANES), jnp.int32)],
        scratch_shapes=[pltpu.VMEM((SUBLANES, LANES), F32)],
        compiler_params=_params(("arbitrary",)),
        name="route_ranks",
    )(idx)


def _invert_kernel(dest_ref, pad_lo_ref, pad_hi_ref, inv_ref, *, n_pairs):
    def fill_segment(seg, c):
        def fill(p, c2):
            inv_ref[p] = n_pairs + lax.rem(p, ROW_BLOCK)
            return c2
        lax.fori_loop(pad_lo_ref[seg], pad_hi_ref[seg], fill, 0)
        return c
    lax.fori_loop(0, pad_lo_ref.shape[0], fill_segment, 0)

    def place(i, c):
        inv_ref[dest_ref[i]] = i
        return c
    lax.fori_loop(0, n_pairs, place, 0, unroll=INVERT_UNROLL)


def _invert(dest, pad_lo, pad_hi, n_rows):
    n_pairs = dest.shape[0]
    grid_spec = pltpu.PrefetchScalarGridSpec(
        num_scalar_prefetch=3, grid=(1,), in_specs=[],
        out_specs=pl.BlockSpec(memory_space=pltpu.SMEM))
    return pl.pallas_call(
        functools.partial(_invert_kernel, n_pairs=n_pairs),
        grid_spec=grid_spec,
        out_shape=jax.ShapeDtypeStruct((n_rows,), jnp.int32),
        compiler_params=_params(("arbitrary",)),
        name="route_invert",
    )(dest, pad_lo, pad_hi)


def _expert_kernel(bexp_ref, nvalid_ref, nexp_ref, nused_ref, inv_ref,
                   h2_hbm, wup_hbm, bup_ref, wdn_hbm, bdn_ref, out_hbm,
                   xbuf, ybuf, xb_buf, act_buf, wup_f, wdn_f, wup_b, wdn_b, gsem, ssem, wsem,
                   *, n_tokens, n_blocks):
    step = pl.program_id(0)
    nused = nused_ref[0]
    tile_rows = ROW_BLOCK * ROW_CHUNKS

    def weight_copies(e):
        return (pltpu.make_async_copy(wup_hbm.at[e], wup_f, wsem.at[0]),
                pltpu.make_async_copy(wdn_hbm.at[e], wdn_f, wsem.at[1]))

    def tile_of(row):
        start = row * ROW_CHUNKS
        if not isinstance(row, int):
            start = pl.multiple_of(start, ROW_CHUNKS)
        return pl.ds(start, ROW_CHUNKS)

    def gather_copy(blk, slot, r, pair=None):
        if pair is None:
            pair = inv_ref[blk * ROW_BLOCK + r]
        tok = pair & (n_tokens - 1)
        return pltpu.make_async_copy(h2_hbm.at[tile_of(tok), :], xbuf.at[slot, tile_of(r), :],
                                     gsem.at[slot])

    def scatter_copy(pair, slot, r):
        return pltpu.make_async_copy(ybuf.at[slot, tile_of(r), :], out_hbm.at[tile_of(pair), :],
                                     ssem.at[slot])

    def wait_gather(slot):
        pltpu.make_async_copy(h2_hbm.at[pl.ds(0, tile_rows), :], xbuf.at[slot], gsem.at[slot]).wait()

    def wait_scatter(slot, nrows):
        @pl.when(nrows > 0)
        def _():
            nr = pl.multiple_of(nrows * ROW_CHUNKS, ROW_CHUNKS)
            pltpu.make_async_copy(ybuf.at[slot, pl.ds(0, nr), :], out_hbm.at[pl.ds(0, nr), :],
                                  ssem.at[slot]).wait()

    n_tiles_e = D_EXPERT // MXU_TILE
    n_tiles_m = D_MODEL // MXU_TILE
    n_stages = 2 * n_tiles_e + n_tiles_m
    copy_stages = (2 * n_stages) // 3
    per_stage = -(-ROW_BLOCK // copy_stages)

    def run_block(i):
        b = step * BLOCKS_PER_STEP + i
        xslot = i % GATHER_SLOTS
        ahead_slot = (i + GATHER_AHEAD) % GATHER_SLOTS
        yslot = i % 2
        prev_yslot = 1 - yslot

        @pl.when(b < nused)
        def _():
            e = bexp_ref[b]
            prev_blk = jnp.maximum(b - 1, 0)
            ahead_blk = jnp.minimum(b + GATHER_AHEAD, n_blocks - 1)
            first_of_expert = jnp.logical_or(b == 0, e != bexp_ref[prev_blk])

            if i == 0:
                @pl.when(step == 0)
                def _():
                    for cp in weight_copies(e):
                        cp.start()
                    for blk in range(GATHER_AHEAD):
                        def start_row(r, c, blk=blk):
                            gather_copy(min(blk, n_blocks - 1), blk, r).start()
                            return c
                        lax.fori_loop(0, ROW_BLOCK, start_row, 0)

            @pl.when(first_of_expert)
            def _():
                for cp in weight_copies(e):
                    cp.wait()
                wup_b[...] = wup_f[...].astype(BF16)
                wdn_b[...] = wdn_f[...].astype(BF16)
                nxt = nexp_ref[b]

                @pl.when(nxt >= 0)
                def _():
                    for cp in weight_copies(nxt):
                        cp.start()

            wait_gather(xslot)

            n_prev = jnp.where(b >= 1, nvalid_ref[prev_blk], 0)
            stage = [0]

            def next_stage_copies():
                st = stage[0]
                stage[0] += 1
                if st < copy_stages:
                    for r in range(st * per_stage, min((st + 1) * per_stage, ROW_BLOCK)):
                        gather_copy(ahead_blk, ahead_slot, r).start(
                            priority=GATHER_DMA_QUEUES[r % len(GATHER_DMA_QUEUES)])
                st2 = st - (n_stages - copy_stages)
                if st2 >= 0:
                    for r in range(st2 * per_stage, min((st2 + 1) * per_stage, ROW_BLOCK)):
                        pair = inv_ref[prev_blk * ROW_BLOCK + r]

                        @pl.when(r < n_prev)
                        def _():
                            scatter_copy(pair, prev_yslot, r).start(
                                priority=SCATTER_DMA_QUEUES[r % len(SCATTER_DMA_QUEUES)])

            for cc in range(ROW_CHUNKS):
                xb_buf[:, cc * LANES:(cc + 1) * LANES] = (
                    xbuf[xslot, pl.ds(cc, ROW_BLOCK, stride=ROW_CHUNKS), :].astype(BF16))

            for j in range(n_tiles_e):
                lo, hi = j * MXU_TILE, (j + 1) * MXU_TILE
                next_stage_copies()
                glu = (jnp.dot(xb_buf[...], wup_b[:, lo:hi], preferred_element_type=F32)
                       + bup_ref[pl.ds(e, 1), lo:hi])
                next_stage_copies()
                lin = (jnp.dot(xb_buf[...], wup_b[:, D_EXPERT + lo:D_EXPERT + hi],
                               preferred_element_type=F32)
                       + bup_ref[pl.ds(e, 1), D_EXPERT + lo:D_EXPERT + hi])
                glu = jnp.minimum(glu, SWIGLU_LIMIT)
                lin = jnp.clip(lin, -SWIGLU_LIMIT, SWIGLU_LIMIT)
                act = glu * _sigmoid(SWIGLU_ALPHA * glu) * (lin + 1.0)
                act_buf[:, lo:hi] = act.astype(BF16)

            @pl.when(b >= 2)
            def _():
                wait_scatter(yslot, nvalid_ref[jnp.maximum(b - 2, 0)])

            for j in range(n_tiles_m):
                lo, hi = j * MXU_TILE, (j + 1) * MXU_TILE
                next_stage_copies()
                y = (jnp.dot(act_buf[...], wdn_b[:, lo:hi], preferred_element_type=F32)
                     + bdn_ref[pl.ds(e, 1), lo:hi])
                for cc in range(MXU_TILE // LANES):
                    ybuf[yslot, pl.ds(lo // LANES + cc, ROW_BLOCK, stride=ROW_CHUNKS), :] = (
                        y[:, cc * LANES:(cc + 1) * LANES])

            @pl.when(b == nused - 1)
            def _():
                n_cur = nvalid_ref[b]

                def start_row(r, c):
                    pair = inv_ref[b * ROW_BLOCK + r]
                    scatter_copy(pair, yslot, r).start(priority=SCATTER_DMA_QUEUES[0])
                    return c
                lax.fori_loop(0, n_cur, start_row, 0)
                wait_scatter(prev_yslot, n_prev)
                wait_scatter(yslot, n_cur)
                for ahead in range(1, GATHER_AHEAD + 1):
                    wait_gather((i + ahead) % GATHER_SLOTS)

    for i in range(BLOCKS_PER_STEP):
        run_block(i)


def _experts(bexp, nvalid, nexp, nused, inv, h2t, w_up, b_up, w_down, b_down):
    n = h2t.shape[0] // ROW_CHUNKS
    assert n & (n - 1) == 0, "token count must be a power of two (pair id = k * n + token)"
    n_blocks = bexp.shape[0]
    assert n_blocks % BLOCKS_PER_STEP == 0
    tile_rows = ROW_BLOCK * ROW_CHUNKS
    whole = lambda s, be, nv, nx, nu, iv: (0, 0)
    grid_spec = pltpu.PrefetchScalarGridSpec(
        num_scalar_prefetch=5,
        grid=(n_blocks // BLOCKS_PER_STEP,),
        in_specs=[pl.BlockSpec(memory_space=pl.ANY),
                  pl.BlockSpec(memory_space=pl.ANY),
                  pl.BlockSpec((N_EXPERTS, 2 * D_EXPERT), whole),
                  pl.BlockSpec(memory_space=pl.ANY),
                  pl.BlockSpec((N_EXPERTS, D_MODEL), whole)],
        out_specs=pl.BlockSpec(memory_space=pl.ANY),
        scratch_shapes=[pltpu.VMEM((GATHER_SLOTS, tile_rows, LANES), F32),
                        pltpu.VMEM((2, tile_rows, LANES), F32),
                        pltpu.VMEM((ROW_BLOCK, D_MODEL), BF16),
                        pltpu.VMEM((ROW_BLOCK, D_EXPERT), BF16),
                        pltpu.VMEM((D_MODEL, 2 * D_EXPERT), F32),
                        pltpu.VMEM((D_EXPERT, D_MODEL), F32),
                        pltpu.VMEM((D_MODEL, 2 * D_EXPERT), BF16),
                        pltpu.VMEM((D_EXPERT, D_MODEL), BF16),
                        pltpu.SemaphoreType.DMA((GATHER_SLOTS,)),
                        pltpu.SemaphoreType.DMA((2,)),
                        pltpu.SemaphoreType.DMA((2,))],
    )
    return pl.pallas_call(
        functools.partial(_expert_kernel, n_tokens=n, n_blocks=n_blocks),
        grid_spec=grid_spec,
        out_shape=jax.ShapeDtypeStruct((n * TOP_K * ROW_CHUNKS, LANES), F32),
        compiler_params=_params(("arbitrary",)),
        name="moe_experts",
    )(bexp, nvalid, nexp, nused, inv, h2t, w_up, b_up, w_down, b_down)


def _combine_kernel(r0_ref, r1_ref, r2_ref, r3_ref, tw_ref, x1_ref, mod_ref, g_ref, o_ref):
    tw = tw_ref[...]
    t = tw.shape[0]
    r_refs = (r0_ref, r1_ref, r2_ref, r3_ref)
    chunks = []
    sumsq = jnp.zeros((t, 1), F32)
    for cc in range(ROW_CHUNKS):
        rows_cc = pl.ds(cc, t, stride=ROW_CHUNKS)
        y = r_refs[0][rows_cc, :] * tw[:, 0:1]
        for kk in range(1, TOP_K):
            y = y + r_refs[kk][rows_cc, :] * tw[:, kk:kk + 1]
        chunks.append(y)
        sumsq = sumsq + jnp.sum(y * y, axis=-1, keepdims=True)
    inv_rms = lax.rsqrt(sumsq * (1.0 / D_MODEL) + RMS_EPS)
    m = mod_ref[0]
    for cc, y in enumerate(chunks):
        lo, hi = cc * LANES, (cc + 1) * LANES
        g_f = m[:, 5 * D_MODEL + lo:5 * D_MODEL + hi]
        o_ref[:, lo:hi] = x1_ref[:, lo:hi] + g_f * (y * inv_rms * g_ref[:, lo:hi])


def _combine(rows, tw, x1, mod3, g, seq):
    n = x1.shape[0]
    tpb = seq // COMBINE_TILE
    nt = n // COMBINE_TILE
    row = lambda i: (i, 0)
    slot_specs = [pl.BlockSpec((COMBINE_TILE * ROW_CHUNKS, LANES),
                               functools.partial(lambda i, kk: (kk * nt + i, 0), kk=kk))
                  for kk in range(TOP_K)]
    return pl.pallas_call(
        _combine_kernel,
        grid=(nt,),
        in_specs=slot_specs + [
            pl.BlockSpec((COMBINE_TILE, LANES), row),
            pl.BlockSpec((COMBINE_TILE, D_MODEL), row),
            pl.BlockSpec((1, 1, N_MOD * D_MODEL), lambda i: (i // tpb, 0, 0)),
            pl.BlockSpec((1, D_MODEL), lambda i: (0, 0))],
        out_specs=pl.BlockSpec((COMBINE_TILE, D_MODEL), row),
        out_shape=jax.ShapeDtypeStruct((n, D_MODEL), F32),
        compiler_params=_params(("arbitrary",)),
        name="moe_combine",
    )(rows, rows, rows, rows, tw, x1, mod3, g)


def _block_diag(w):
    g, d, _ = w.shape
    eye = jnp.eye(g, dtype=w.dtype)
    return jnp.einsum("gij,gh->gihj", w, eye).reshape(g * d, g * d)


def _layer(x2d, mod, bsz, seq, norm_mix_pre, norm_mix_post, w_in, conv_w, conv_b, lru_wa, lru_ba,
           lru_wx, lru_bx, lru_lambda, attn_fb, gn_lru, gn_attn, w_out, norm_ffn_pre, norm_ffn_post,
           w_router, b_router, w_up, b_up, w_down, b_down):
    n = x2d.shape[0]
    mod3 = mod.reshape(bsz, 1, N_MOD * D_MODEL)
    d_in = w_in.shape[1]
    w_in_b = jnp.pad(w_in, ((0, 0), (0, 2 * D_LRU + 3 * D_ATTN + LANES - d_in))).astype(BF16)
    xl, yl, q, k, vt, f = _inproj(x2d, mod3, norm_mix_pre.reshape(1, -1), w_in_b, seq)

    wg = jnp.concatenate([_block_diag(lru_wa), _block_diag(lru_wx)], axis=1).astype(BF16)
    bg = jnp.concatenate([lru_ba.reshape(1, -1), lru_bx.reshape(1, -1)], axis=1)
    fb = jnp.pad(attn_fb.reshape(1, -1), ((0, 0), (0, LANES - ATTN_HEADS)))
    lru_n, ccol = _lru(xl, yl, f, conv_w, conv_b.reshape(1, -1), wg, bg,
                       lru_lambda.reshape(1, -1), fb, gn_lru.reshape(1, -1), bsz, seq)

    attn = _attention(q, k, vt, ccol, bsz, seq)

    wr_f = jnp.pad(w_router, ((0, 0), (0, LANES - N_EXPERTS)))
    wr_hi = wr_f.astype(BF16)
    wr = jnp.concatenate([wr_hi, (wr_f - wr_hi.astype(F32)).astype(BF16)], axis=1)
    br =jnp.pad(b_router.reshape(1, -1), ((0, 0), (0, LANES - N_EXPERTS)), constant_values=NEG_BIG)
    x1, h2, idx, tw = _outproj(lru_n, attn, x2d, mod3, gn_attn.reshape(1, -1), w_out.astype(BF16),
                               norm_mix_post.reshape(1, -1), norm_ffn_pre.reshape(1, -1), wr, br, seq)

    rank, cnt = _ranks(idx)
    counts = cnt[0, :N_EXPERTS]
    padded = (counts + ROW_BLOCK - 1) // ROW_BLOCK * ROW_BLOCK
    pends = jnp.cumsum(padded)
    pstarts = pends - padded
    n_blocks = n * TOP_K // ROW_BLOCK + N_EXPERTS
    n_rows = n_blocks * ROW_BLOCK
    blk_start = jnp.arange(n_blocks, dtype=jnp.int32) * ROW_BLOCK
    eids = jnp.arange(N_EXPERTS, dtype=jnp.int32)
    bexp = jnp.minimum(jnp.sum((pends[None, :] <= blk_start[:, None]).astype(jnp.int32), axis=1),
                       N_EXPERTS - 1)
    onehot_b = bexp[:, None] == eids[None, :]
    blk_count = jnp.sum(jnp.where(onehot_b, counts[None, :], 0), axis=1)
    blk_pstart = jnp.sum(jnp.where(onehot_b, pstarts[None, :], 0), axis=1)
    nused = (pends[-1] // ROW_BLOCK).astype(jnp.int32)
    nvalid = jnp.clip(blk_count - (blk_start - blk_pstart), 0, ROW_BLOCK)
    nvalid = jnp.where(jnp.arange(n_blocks) < nused, nvalid, 0).astype(jnp.int32)
    later_used = (counts[None, :] > 0) & (eids[None, :] > eids[:, None])
    next_used = jnp.min(jnp.where(later_used, eids[None, :], N_EXPERTS), axis=1)
    next_used = jnp.where(next_used == N_EXPERTS, -1, next_used)
    nexp = jnp.sum(jnp.where(onehot_b, next_used[None, :], 0), axis=1).astype(jnp.int32)

    e_flat = idx[:, :TOP_K].T.reshape(-1)
    pair_pstart = jnp.sum(jnp.where(e_flat[:, None] == eids[None, :], pstarts[None, :], 0), axis=1)
    dest = pair_pstart + rank[:, :TOP_K].T.reshape(-1)
    pad_lo = jnp.concatenate([pstarts + counts, pends[-1:]]).astype(jnp.int32)
    pad_hi = jnp.concatenate([pends, jnp.full((1,), n_rows, jnp.int32)]).astype(jnp.int32)
    inv = _invert(dest.astype(jnp.int32), pad_lo, pad_hi, n_rows)

    rows = _experts(bexp, nvalid, nexp, nused.reshape(1), inv, h2, w_up, b_up, w_down, b_down)
    return _combine(rows, tw, x1, mod3, norm_ffn_post.reshape(1, -1), seq)


def kernel(x, c, w_ada, b_ada, norm_mix_pre, norm_mix_post, w_in, conv_w, conv_b, lru_wa, lru_ba,
           lru_wx, lru_bx, lru_lambda, attn_fb, gn_lru, gn_attn, w_out, norm_ffn_pre, norm_ffn_post,
           w_router, b_router, w_up, b_up, w_down, b_down):
    bsz, seq, d = x.shape
    x2d = x.reshape(bsz * seq, d)
    for l in range(w_ada.shape[0]):
        mod = _ada(c, w_ada[l], b_ada[l])
        x2d = _layer(x2d, mod, bsz, seq, norm_mix_pre[l], norm_mix_post[l], w_in[l], conv_w[l],
                     conv_b[l], lru_wa[l], lru_ba[l], lru_wx[l], lru_bx[l], lru_lambda[l],
                     attn_fb[l], gn_lru[l], gn_attn[l], w_out[l], norm_ffn_pre[l], norm_ffn_post[l],
                     w_router[l], b_router[l], w_up[l], b_up[l], w_down[l], b_down[l])
    return x2d.reshape(bsz, seq, d)
```
